```python
import jax, jax.numpy as jnp
from jax import lax
import numpy as np

D_MODEL = 1024
BATCH = 32
SEQ = 2048
DEPTH = 2

CHUNK = 64
Q_BLOCK = 128
N_BRANCH = 3

POOL_WINDOWS = (2, 4, 8, 16)
POOL_GROUPS = 4
POOL_WIDTH = D_MODEL
POOL_GW = POOL_WIDTH // POOL_GROUPS

MLA_HEADS = 8
QK_NOPE = 128
QK_ROPE = 64
V_DIM = 128
Q_LORA = D_MODEL // 4
KV_LORA = D_MODEL // 8
MLA_WIDTH = MLA_HEADS * V_DIM
ROPE_THETA = 10000.0

SG_WIDTH = D_MODEL
SG_BLOCK = 128
SG_GROUPS = 8
SG_GW = SG_WIDTH // SG_GROUPS

D_FF = 2816
CONV_W = 3

OFF_POOL = N_BRANCH * D_MODEL
OFF_CQ = OFF_POOL + POOL_WIDTH
OFF_CKV = OFF_CQ + Q_LORA
OFF_KR = OFF_CKV + KV_LORA
OFF_SG = OFF_KR + QK_ROPE
IN_WIDTH = OFF_SG + 2 * SG_WIDTH

ALPHA = (2 * DEPTH) ** 0.25
BETA = (8 * DEPTH) ** -0.25
LN_EPS = 1e-5
RMS_EPS = 1e-6

kernel_name = "hybrid_pool_mla_sgu_convffn_deepnorm"


def _layer_norm(x):
    xf = x.astype(jnp.float32)
    mu = jnp.mean(xf, axis=-1, keepdims=True)
    var = jnp.mean(jnp.square(xf - mu), axis=-1, keepdims=True)
    return ((xf - mu) * lax.rsqrt(var + LN_EPS)).astype(x.dtype)


def _rms_norm(x, g):
    xf = x.astype(jnp.float32)
    y = xf * lax.rsqrt(jnp.mean(jnp.square(xf), axis=-1, keepdims=True) + RMS_EPS)
    return y.astype(x.dtype) * g


def _rope_tables(pos):
    freqs = ROPE_THETA ** (-jnp.arange(0, QK_ROPE, 2, dtype=jnp.float32) / QK_ROPE)
    ang = pos.astype(jnp.float32)[..., None] * freqs
    return jnp.cos(ang), jnp.sin(ang)


def _apply_rope(x, cos, sin):
    xf = x.astype(jnp.float32)
    x1, x2 = jnp.split(xf, 2, axis=-1)
    out = jnp.concatenate([x1 * cos - x2 * sin, x2 * cos + x1 * sin], axis=-1)
    return out.astype(x.dtype)


def _modulate(x, shift, scale):
    return _layer_norm(x) * (1.0 + scale[:, None, :]) + shift[:, None, :]


def _pool_mixer(a, w_pool, s_pool):
    B, S, _ = a.shape
    af = a.astype(jnp.float32)
    cs = jnp.pad(jnp.cumsum(af, axis=1), ((0, 0), (1, 0), (0, 0)))
    outs = []
    for g, w in enumerate(POOL_WINDOWS):
        c_g = cs[..., g * POOL_GW:(g + 1) * POOL_GW]
        lag = jnp.pad(c_g, ((0, 0), (w, 0), (0, 0)))[:, :S + 1]
        cnt = jnp.minimum(jnp.arange(1, S + 1), w).astype(jnp.float32)[None, :, None]
        mean = (c_g[:, 1:] - lag[:, 1:]) / cnt
        outs.append(mean - af[..., g * POOL_GW:(g + 1) * POOL_GW])
    pooled = jnp.stack(outs, axis=2).astype(a.dtype)
    mixed = jnp.einsum('bsgc,gcd->bsgd', pooled, w_pool)
    return mixed.reshape(B, S, POOL_WIDTH) * s_pool


def _mla(cq, ckv, kr, cos, sin, g_q, w_uq, g_kv, w_ukv):
    B, S, _ = cq.shape
    q = jnp.einsum('bsr,rhd->bshd', _rms_norm(cq, g_q), w_uq)
    kv = jnp.einsum('bsr,rhd->bshd', _rms_norm(ckv, g_kv), w_ukv)
    q_nope, q_rope = q[..., :QK_NOPE], q[..., QK_NOPE:]
    k_nope, v = kv[..., :QK_NOPE], kv[..., QK_NOPE:]
    q_rope = _apply_rope(q_rope, cos[:, :, None, :], sin[:, :, None, :])
    k_rope = _apply_rope(kr, cos, sin)
    scale = (QK_NOPE + QK_ROPE) ** -0.5
    neg = jnp.finfo(jnp.float32).min
    outs = []
    for i in range(S // Q_BLOCK):
        q0, q1 = i * Q_BLOCK, (i + 1) * Q_BLOCK
        s = (jnp.einsum('bqhd,bkhd->bhqk', q_nope[:, q0:q1], k_nope[:, :q1])
             + jnp.einsum('bqhd,bkd->bhqk', q_rope[:, q0:q1], k_rope[:, :q1]))
        s = s.astype(jnp.float32) * scale
        qc = (q0 + jnp.arange(Q_BLOCK)) // CHUNK
        kc = jnp.arange(q1) // CHUNK
        s = jnp.where(kc[None, :] <= qc[:, None], s, neg)
        p = jax.nn.softmax(s, axis=-1).astype(v.dtype)
        outs.append(jnp.einsum('bhqk,bkhd->bqhd', p, v[:, :q1]))
    o = jnp.concatenate(outs, axis=1)
    return o.reshape(B, S, MLA_WIDTH)


def _sgu(uv, g_sg, b_sg, w_s, b_s):
    uv = jax.nn.gelu(uv, approximate=False)
    u, v = jnp.split(uv, 2, axis=-1)
    v = _layer_norm(v) * g_sg + b_sg
    B, S, _ = v.shape
    nb = S // SG_BLOCK
    v = v.reshape(B, nb, SG_BLOCK, SG_GROUPS, SG_GW)
    mask = jnp.tril(jnp.ones((SG_BLOCK, SG_BLOCK), dtype=bool))
    ws = jnp.where(mask[None], w_s, jnp.zeros_like(w_s))
    z = jnp.einsum('gts,bnsgc->bntgc', ws, v) + b_s[:, :, None]
    return u * z.reshape(B, S, SG_WIDTH)


def _token_mixer(h, cos, sin, w_in, w_pool, s_pool, g_q, w_uq, g_kv, w_ukv,
                 g_sg, b_sg, w_s, b_s, w_branch, w_o):
    B, S, _ = h.shape
    p = h @ w_in
    gates, a, cq, ckv, kr, uv = jnp.split(p, [OFF_POOL, OFF_CQ, OFF_CKV, OFF_KR, OFF_SG], axis=-1)
    gates = jax.nn.sigmoid(gates).reshape(B, S, N_BRANCH, D_MODEL)
    y_a = _pool_mixer(a, w_pool, s_pool) @ w_branch[0]
    y_b = _mla(cq, ckv, kr, cos, sin, g_q, w_uq, g_kv, w_ukv) @ w_branch[1]
    y_c = _sgu(uv, g_sg, b_sg, w_s, b_s) @ w_branch[2]
    merged = gates[:, :, 0] * y_a + gates[:, :, 1] * y_b + gates[:, :, 2] * y_c
    return merged @ w_o


def _conv_ffn(h, w_up, conv_w, conv_b, w_down):
    z = h @ w_up
    S = z.shape[1]
    zp = jnp.pad(z, ((0, 0), (CONV_W - 1, 0), (0, 0)))
    zc = conv_b + conv_w[0] * zp[:, 0:S]
    for k in range(1, CONV_W):
        zc = zc + conv_w[k] * zp[:, k:k + S]
    val, gate = jnp.split(zc, 2, axis=-1)
    return (jax.nn.silu(gate) * val) @ w_down


def setup_inputs(seed: int = 0) -> dict:
    key = jax.random.key(seed)
    ks = jax.random.split(key, 26)
    f32 = jnp.float32

    def nrm(k, shape, scale):
        return jax.random.normal(k, shape, f32) * scale

    L = DEPTH
    pos = (jax.random.randint(ks[2], (BATCH, 1), 0, 8192, dtype=jnp.int32)
           + jnp.arange(SEQ, dtype=jnp.int32)[None, :])
    return {
        "x": nrm(ks[0], (BATCH, SEQ, D_MODEL), 1.0),
        "c": nrm(ks[1], (BATCH, D_MODEL), 1.0),
        "pos": pos,
        "w_ada": nrm(ks[3], (L, D_MODEL, 6 * D_MODEL), 0.5 * D_MODEL ** -0.5),
        "b_ada": nrm(ks[4], (L, 6 * D_MODEL), 0.02),
        "w_in": nrm(ks[5], (L, D_MODEL, IN_WIDTH), D_MODEL ** -0.5),
        "w_pool": nrm(ks[6], (L, POOL_GROUPS, POOL_GW, POOL_GW), POOL_GW ** -0.5),
        "s_pool": 1.0 + nrm(ks[7], (L, POOL_WIDTH), 0.05),
        "g_q": 1.0 + nrm(ks[8], (L, Q_LORA), 0.05),
        "w_uq": nrm(ks[9], (L, Q_LORA, MLA_HEADS, QK_NOPE + QK_ROPE), Q_LORA ** -0.5),
        "g_kv": 1.0 + nrm(ks[10], (L, KV_LORA), 0.05),
        "w_ukv": nrm(ks[11], (L, KV_LORA, MLA_HEADS, QK_NOPE + V_DIM), KV_LORA ** -0.5),
        "g_sg": 1.0 + nrm(ks[12], (L, SG_WIDTH), 0.05),
        "b_sg": nrm(ks[13], (L, SG_WIDTH), 0.02),
        "w_s": nrm(ks[14], (L, SG_GROUPS, SG_BLOCK, SG_BLOCK), SG_BLOCK ** -0.5),
        "b_s": 1.0 + nrm(ks[15], (L, SG_BLOCK, SG_GROUPS), 0.02),
        "w_branch": nrm(ks[16], (L, N_BRANCH, D_MODEL, D_MODEL), D_MODEL ** -0.5),
        "w_o": nrm(ks[17], (L, D_MODEL, D_MODEL), BETA * D_MODEL ** -0.5),
        "ln_t_g": 1.0 + nrm(ks[18], (L, D_MODEL), 0.05),
        "ln_t_b": nrm(ks[19], (L, D_MODEL), 0.02),
        "w_up": nrm(ks[20], (L, D_MODEL, 2 * D_FF), D_MODEL ** -0.5),
        "conv_w": nrm(ks[21], (L, CONV_W, 2 * D_FF), CONV_W ** -0.5),
        "conv_b": nrm(ks[22], (L, 2 * D_FF), 0.02),
        "w_down": nrm(ks[23], (L, D_FF, D_MODEL), BETA * D_FF ** -0.5),
        "ln_f_g": 1.0 + nrm(ks[24], (L, D_MODEL), 0.05),
        "ln_f_b": nrm(ks[25], (L, D_MODEL), 0.02),
    }


def reference(x, c, pos, w_ada, b_ada, w_in, w_pool, s_pool, g_q, w_uq, g_kv, w_ukv,
              g_sg, b_sg, w_s, b_s, w_branch, w_o, ln_t_g, ln_t_b,
              w_up, conv_w, conv_b, w_down, ln_f_g, ln_f_b):
    cos, sin = _rope_tables(pos)
    c_act = jax.nn.silu(c)
    for l in range(DEPTH):
        mod = c_act @ w_ada[l] + b_ada[l]
        sh_t, sc_t, gt_t, sh_f, sc_f, gt_f = jnp.split(mod, 6, axis=-1)
        h = _modulate(x, sh_t, sc_t)
        y = _token_mixer(h, cos, sin, w_in[l], w_pool[l], s_pool[l], g_q[l], w_uq[l],
                         g_kv[l], w_ukv[l], g_sg[l], b_sg[l], w_s[l], b_s[l],
                         w_branch[l], w_o[l])
        x = _layer_norm(ALPHA * x + gt_t[:, None, :] * y) * ln_t_g[l] + ln_t_b[l]
        h = _modulate(x, sh_f, sc_f)
        y = _conv_ffn(h, w_up[l], conv_w[l], conv_b[l], w_down[l])
        x = _layer_norm(ALPHA * x + gt_f[:, None, :] * y) * ln_f_g[l] + ln_f_b[l]
    return x
```

```python
import functools

import jax
import jax.numpy as jnp
from jax import lax
from jax.experimental import pallas as pl
from jax.experimental.pallas import tpu as pltpu

F32 = jnp.float32
BF16 = jnp.bfloat16

D_MODEL = 1024
CHUNK = 64
N_BRANCH = 3
POOL_WINDOWS = (2, 4, 8, 16)
POOL_GW = D_MODEL // len(POOL_WINDOWS)
POOL_HALO = 16
MLA_HEADS = 8
QK_NOPE = 128
QK_ROPE = 64
V_DIM = 128
Q_LORA = D_MODEL // 4
KV_LORA = D_MODEL // 8
ROPE_THETA = 10000.0
SG_BLOCK = 128
SG_GROUPS = 8
SG_GW = D_MODEL // SG_GROUPS
D_FF = 2816
CONV_W = 3
CONV_HALO = 8
LN_EPS = 1e-5
RMS_EPS = 1e-6
ATT_SCALE = (QK_NOPE + QK_ROPE) ** -0.5
MASK_VALUE = -1e30

C_GATE = 0
C_POOL = C_GATE + N_BRANCH * D_MODEL
C_LAT = C_POOL + D_MODEL
LAT_W = Q_LORA + KV_LORA + 4 * QK_ROPE
C_UV = C_LAT + LAT_W
IN_W = C_UV + 2 * D_MODEL

VMEM_LIMIT_BYTES = 60 * 1024 * 1024

TS_IN = 256
TQ_ATT = 256
TM_OUT = 512
TS_FFN = 256


def _layer_norm(x):
    mu = jnp.mean(x, axis=-1, keepdims=True)
    xc = x - mu
    var = jnp.mean(xc * xc, axis=-1, keepdims=True)
    return xc * lax.rsqrt(var + LN_EPS)


def _rms_norm(x, g):
    return x * lax.rsqrt(jnp.mean(x * x, axis=-1, keepdims=True) + RMS_EPS) * g


def _dot(a, b):
    return jnp.dot(a, b, preferred_element_type=F32)


def _const_spec(shape):
    nd = len(shape)
    return pl.BlockSpec(shape, lambda *_: (0,) * nd, pipeline_mode=pl.Buffered(1))


def _mod_kernel(c_ref, w_ref, b_ref, o_ref):
    c = c_ref[...]
    c_act = c * jax.nn.sigmoid(c)
    o_ref[...] = jnp.dot(c_act, w_ref[...], preferred_element_type=F32,
                         precision=lax.Precision.HIGHEST) + b_ref[...]


def _modulation(c, w_ada, b_ada):
    depth, d, six_d = w_ada.shape
    nb = c.shape[0]
    out = pl.pallas_call(
        _mod_kernel,
        grid=(depth, six_d // d),
        in_specs=[
            pl.BlockSpec((nb, d), lambda l, j: (0, 0)),
            pl.BlockSpec((None, d, d), lambda l, j: (l, 0, j)),
            pl.BlockSpec((None, 1, d), lambda l, j: (l, 0, j)),
        ],
        out_specs=pl.BlockSpec((None, nb, d), lambda l, j: (l, 0, j)),
        out_shape=jax.ShapeDtypeStruct((depth, nb, six_d), F32),
        name="adaln_mod",
    )(c, w_ada, b_ada.reshape(depth, 1, six_d))
    return out.reshape(depth, nb, 6, d)


def _mixer_in_kernel(x_ref, mod_ref, cos_ref, sin_ref, w_in_ref, w_pool_ref, s_pool_ref,
                     g_q_ref, w_qn_ref, w_qr_ref, w_qrs_ref, w_ukt_ref, g_kv_ref,
                     g_sg_ref, b_sg_ref, w_s_ref, b_s_ref,
                     gates_ref, ya_ref, q_ref, k_ref, yc_ref, abuf):
    ts = x_ref.shape[0]
    s_idx = pl.program_id(1)

    h = _layer_norm(x_ref[...]) * (1.0 + mod_ref[1:2, :]) + mod_ref[0:1, :]
    hb = h.astype(BF16)

    gates_ref[...] = jax.nn.sigmoid(
        _dot(hb, w_in_ref[:, C_GATE:C_GATE + N_BRANCH * D_MODEL])).astype(BF16)

    @pl.when(s_idx == 0)
    def _():
        abuf[0:POOL_HALO, :] = jnp.zeros((POOL_HALO, D_MODEL), F32)

    abuf[POOL_HALO:POOL_HALO + ts, :] = _dot(hb, w_in_ref[:, C_POOL:C_POOL + D_MODEL])
    pos = lax.broadcasted_iota(jnp.int32, (ts, 1), 0) + s_idx * ts
    for g, w in enumerate(POOL_WINDOWS):
        cols = slice(g * POOL_GW, (g + 1) * POOL_GW)
        cur = abuf[POOL_HALO:POOL_HALO + ts, cols]
        win = cur
        for k in range(1, w):
            win = win + abuf[POOL_HALO - k:POOL_HALO - k + ts, cols]
        cnt = jnp.minimum(pos + 1, w).astype(F32)
        pooled = win / cnt - cur
        mixed = _dot(pooled.astype(BF16), w_pool_ref[g])
        ya_ref[:, cols] = (mixed * s_pool_ref[:, cols]).astype(BF16)
    abuf[0:POOL_HALO, :] = abuf[ts:ts + POOL_HALO, :]

    lat = _dot(hb, w_in_ref[:, C_LAT:C_LAT + LAT_W])
    cos = cos_ref[...]
    sin = sin_ref[...]
    cqn = _rms_norm(lat[:, 0:Q_LORA], g_q_ref[...]).astype(BF16)
    ckvn = _rms_norm(lat[:, Q_LORA:Q_LORA + KV_LORA], g_kv_ref[...])
    o_kr = Q_LORA + KV_LORA
    kr = lat[:, o_kr:o_kr + 2 * QK_ROPE] * cos + lat[:, o_kr + 2 * QK_ROPE:o_kr + 4 * QK_ROPE] * sin
    k_ref[:, 0:KV_LORA] = ckvn.astype(BF16)
    k_ref[:, KV_LORA:KV_LORA + 2 * QK_ROPE] = kr.astype(BF16)

    qn = _dot(cqn, w_qn_ref[...])
    for hd in range(MLA_HEADS):
        cols = slice(hd * QK_NOPE, (hd + 1) * QK_NOPE)
        q_lat = _dot(qn[:, cols].astype(BF16), w_ukt_ref[hd]) * ATT_SCALE
        q_ref[:, cols] = q_lat.astype(BF16)
    reps = MLA_HEADS * QK_ROPE // cos.shape[1]
    cos_t = jnp.concatenate([cos] * reps, axis=1)
    sin_t = jnp.concatenate([sin] * reps, axis=1)
    q_rope = (_dot(cqn, w_qr_ref[...]) * cos_t + _dot(cqn, w_qrs_ref[...]) * sin_t) * ATT_SCALE
    q_ref[:, MLA_HEADS * QK_NOPE:] = q_rope.astype(BF16)

    uv = _dot(hb, w_in_ref[:, C_UV:C_UV + 2 * D_MODEL])
    uv = 0.5 * uv * (1.0 + lax.erf(uv * (2.0 ** -0.5)))
    u = uv[:, :D_MODEL]
    v = (_layer_norm(uv[:, D_MODEL:]) * g_sg_ref[...] + b_sg_ref[...]).astype(BF16)
    row = lax.broadcasted_iota(jnp.int32, (SG_BLOCK, SG_BLOCK), 0)
    col = lax.broadcasted_iota(jnp.int32, (SG_BLOCK, SG_BLOCK), 1)
    n_blk = ts // SG_BLOCK
    for g in range(SG_GROUPS):
        cols = slice(g * SG_GW, (g + 1) * SG_GW)
        w_tril = jnp.where(col <= row, w_s_ref[g], 0.0).astype(BF16)
        v_cat = jnp.concatenate(
            [v[n * SG_BLOCK:(n + 1) * SG_BLOCK, cols] for n in range(n_blk)], axis=1)
        z = _dot(w_tril, v_cat)
        for n in range(n_blk):
            rows = slice(n * SG_BLOCK, (n + 1) * SG_BLOCK)
            zn = z[:, n * SG_GW:(n + 1) * SG_GW] + b_s_ref[:, cols]
            yc_ref[rows, cols] = (u[rows, cols] * zn).astype(BF16)


def _mixer_in(x, mod_l, cos, sin, w):
    nb, seq, d = x.shape
    ts = TS_IN
    tile = lambda width: pl.BlockSpec((None, ts, width), lambda b, s: (b, s, 0))
    consts = [w["w_in"], w["w_pool"], w["s_pool"], w["g_q"], w["w_qn"], w["w_qr"], w["w_qrs"],
              w["w_ukt"], w["g_kv"], w["g_sg"], w["b_sg"], w["w_s"], w["b_s"]]
    q_w = MLA_HEADS * (QK_NOPE + QK_ROPE)
    k_w = KV_LORA + 2 * QK_ROPE
    return pl.pallas_call(
        _mixer_in_kernel,
        grid=(nb, seq // ts),
        in_specs=[tile(d),
                  pl.BlockSpec((None, 6, d), lambda b, s: (b, 0, 0)),
                  tile(cos.shape[-1]), tile(sin.shape[-1])]
                 + [_const_spec(a.shape) for a in consts],
        out_specs=[tile(N_BRANCH * d), tile(d), tile(q_w), tile(k_w), tile(d)],
        out_shape=[jax.ShapeDtypeStruct((nb, seq, N_BRANCH * d), BF16),
                   jax.ShapeDtypeStruct((nb, seq, d), BF16),
                   jax.ShapeDtypeStruct((nb, seq, q_w), BF16),
                   jax.ShapeDtypeStruct((nb, seq, k_w), BF16),
                   jax.ShapeDtypeStruct((nb, seq, d), BF16)],
        scratch_shapes=[pltpu.VMEM((ts + POOL_HALO, d), F32)],
        compiler_params=pltpu.CompilerParams(
            dimension_semantics=("arbitrary", "arbitrary"),
            vmem_limit_bytes=VMEM_LIMIT_BYTES),
        name="mixer_in",
    )(x, mod_l, cos, sin, *consts)


def _attention_kernel(q_ref, k_ref, w_uv_ref, o_ref, q_scr, m_scr, l_scr, acc_scr):
    tq = q_ref.shape[0]
    tk = tq
    rows = MLA_HEADS * tq
    i = pl.program_id(1)

    lane = lax.broadcasted_iota(jnp.int32, (tq, 2 * QK_ROPE), 1)
    for hd in range(MLA_HEADS):
        pair = q_ref[:, MLA_HEADS * QK_NOPE + (hd // 2) * 2 * QK_ROPE:
                     MLA_HEADS * QK_NOPE + (hd // 2 + 1) * 2 * QK_ROPE]
        keep = (lane < QK_ROPE) if hd % 2 == 0 else (lane >= QK_ROPE)
        q_scr[hd * tq:(hd + 1) * tq, 0:KV_LORA] = q_ref[:, hd * QK_NOPE:(hd + 1) * QK_NOPE]
        q_scr[hd * tq:(hd + 1) * tq, KV_LORA:] = jnp.where(keep, pair, jnp.zeros_like(pair))

    m_scr[...] = jnp.full((rows, 1), MASK_VALUE, F32)
    l_scr[...] = jnp.zeros((rows, 1), F32)
    acc_scr[...] = jnp.zeros((rows, KV_LORA), F32)

    def step(j, masked):
        kb = k_ref[pl.ds(pl.multiple_of(j * tk, tk), tk), :]
        s = lax.dot_general(q_scr[...], kb, (((1,), (1,)), ((), ())),
                            preferred_element_type=F32)
        if masked:
            shift = CHUNK.bit_length() - 1
            q_chunk = (lax.broadcasted_iota(jnp.int32, (rows, tk), 0) & (tq - 1)) >> shift
            k_chunk = lax.broadcasted_iota(jnp.int32, (rows, tk), 1) >> shift
            s = jnp.where(k_chunk <= q_chunk, s, MASK_VALUE)
        m_prev = m_scr[...]
        m_new = jnp.maximum(m_prev, jnp.max(s, axis=1, keepdims=True))
        alpha = jnp.exp(m_prev - m_new)
        p = jnp.exp(s - m_new)
        l_scr[...] = alpha * l_scr[...] + jnp.sum(p, axis=1, keepdims=True)
        acc_scr[...] = alpha * acc_scr[...] + _dot(p.astype(BF16), kb[:, 0:KV_LORA])
        m_scr[...] = m_new

    def body(j, carry):
        step(j, masked=False)
        return carry

    lax.fori_loop(0, i, body, 0)
    step(i, masked=True)

    o_lat = acc_scr[...] / l_scr[...]
    for hd in range(MLA_HEADS):
        o_h = _dot(o_lat[hd * tq:(hd + 1) * tq, :].astype(BF16), w_uv_ref[hd])
        o_ref[:, hd * V_DIM:(hd + 1) * V_DIM] = o_h.astype(BF16)


def _attention(q, k, w_uv):
    nb, seq, q_w = q.shape
    k_w = k.shape[-1]
    tq = TQ_ATT
    rows = MLA_HEADS * tq
    return pl.pallas_call(
        _attention_kernel,
        grid=(nb, seq // tq),
        in_specs=[pl.BlockSpec((None, tq, q_w), lambda b, i: (b, i, 0)),
                  pl.BlockSpec((None, seq, k_w), lambda b, i: (b, 0, 0)),
                  _const_spec(w_uv.shape)],
        out_specs=pl.BlockSpec((None, tq, MLA_HEADS * V_DIM), lambda b, i: (b, i, 0)),
        out_shape=jax.ShapeDtypeStruct((nb, seq, MLA_HEADS * V_DIM), BF16),
        scratch_shapes=[pltpu.VMEM((rows, k_w), BF16),
                        pltpu.VMEM((rows, 1), F32),
                        pltpu.VMEM((rows, 1), F32),
                        pltpu.VMEM((rows, KV_LORA), F32)],
        compiler_params=pltpu.CompilerParams(
            dimension_semantics=("arbitrary", "arbitrary"),
            vmem_limit_bytes=VMEM_LIMIT_BYTES),
        name="latent_attention",
    )(q, k, w_uv)


def _mixer_out_kernel(x_ref, mod_ref, gates_ref, ya_ref, yb_ref, yc_ref, w_br_ref, w_o_ref,
                      ln_g_ref, ln_b_ref, o_ref, *, alpha):
    d = x_ref.shape[1]
    merged = None
    for br, y_ref in enumerate((ya_ref, yb_ref, yc_ref)):
        y = _dot(y_ref[...], w_br_ref[br])
        term = gates_ref[:, br * d:(br + 1) * d].astype(F32) * y
        merged = term if merged is None else merged + term
    y = _dot(merged.astype(BF16), w_o_ref[...])
    r = alpha * x_ref[...] + mod_ref[2:3, :] * y
    o_ref[...] = _layer_norm(r) * ln_g_ref[...] + ln_b_ref[...]


def _mixer_out(x, mod_l, gates, ya, yb, yc, w_br, w_o, ln_g, ln_b, alpha):
    nb, seq, d = x.shape
    tm = TM_OUT
    tile = lambda width: pl.BlockSpec((None, tm, width), lambda b, s: (b, s, 0))
    return pl.pallas_call(
        functools.partial(_mixer_out_kernel, alpha=alpha),
        grid=(nb, seq // tm),
        in_specs=[tile(d),
                  pl.BlockSpec((None, 6, d), lambda b, s: (b, 0, 0)),
                  tile(N_BRANCH * d), tile(d), tile(d), tile(d),
                  _const_spec(w_br.shape), _const_spec(w_o.shape),
                  _const_spec(ln_g.shape), _const_spec(ln_b.shape)],
        out_specs=tile(d),
        out_shape=jax.ShapeDtypeStruct((nb, seq, d), F32),
        compiler_params=pltpu.CompilerParams(
            dimension_semantics=("arbitrary", "arbitrary"),
            vmem_limit_bytes=VMEM_LIMIT_BYTES),
        name="mixer_out",
    )(x, mod_l, gates, ya, yb, yc, w_br, w_o, ln_g, ln_b)


def _conv_ffn_kernel(x_ref, mod_ref, w_up_ref, cw_ref, cb_ref, w_down_ref, ln_g_ref, ln_b_ref,
                     o_ref, zbuf, *, alpha):
    ts = x_ref.shape[0]
    s_idx = pl.program_id(1)
    x = x_ref[...]
    h = (_layer_norm(x) * (1.0 + mod_ref[4:5, :]) + mod_ref[3:4, :]).astype(BF16)

    @pl.when(s_idx == 0)
    def _():
        zbuf[0:CONV_HALO, :] = jnp.zeros((CONV_HALO, 2 * D_FF), F32)

    zbuf[CONV_HALO:CONV_HALO + ts, :] = _dot(h, w_up_ref[...])
    zc = cb_ref[...]
    for k in range(CONV_W):
        off = CONV_HALO - (CONV_W - 1) + k
        zc = zc + cw_ref[k:k + 1, :] * zbuf[off:off + ts, :]
    zbuf[0:CONV_HALO, :] = zbuf[ts:ts + CONV_HALO, :]

    val = zc[:, :D_FF]
    gate = zc[:, D_FF:]
    act = (gate * jax.nn.sigmoid(gate) * val).astype(BF16)
    y = _dot(act, w_down_ref[...])
    r = alpha * x + mod_ref[5:6, :] * y
    o_ref[...] = _layer_norm(r) * ln_g_ref[...] + ln_b_ref[...]


def _conv_ffn(x, mod_l, w_up, conv_w, conv_b, w_down, ln_g, ln_b, alpha):
    nb, seq, d = x.shape
    ts = TS_FFN
    tile = pl.BlockSpec((None, ts, d), lambda b, s: (b, s, 0))
    consts = [w_up, conv_w, conv_b, w_down, ln_g, ln_b]
    return pl.pallas_call(
        functools.partial(_conv_ffn_kernel, alpha=alpha),
        grid=(nb, seq // ts),
        in_specs=[tile, pl.BlockSpec((None, 6, d), lambda b, s: (b, 0, 0))]
                 + [_const_spec(a.shape) for a in consts],
        out_specs=tile,
        out_shape=jax.ShapeDtypeStruct((nb, seq, d), F32),
        scratch_shapes=[pltpu.VMEM((ts + CONV_HALO, 2 * D_FF), F32)],
        compiler_params=pltpu.CompilerParams(
            dimension_semantics=("arbitrary", "arbitrary"),
            vmem_limit_bytes=VMEM_LIMIT_BYTES),
        name="conv_ffn",
    )(x, mod_l, w_up, conv_w, conv_b, w_down, ln_g, ln_b)


def _swap_halves(w):
    half = w.shape[-1] // 2
    return jnp.concatenate([w[..., half:], w[..., :half]], axis=-1)


def _rope_tables(pos):
    freqs = ROPE_THETA ** (-jnp.arange(0, QK_ROPE, 2, dtype=F32) / QK_ROPE)
    ang = pos.astype(F32)[..., None] * freqs
    cos, sin = jnp.cos(ang), jnp.sin(ang)
    return (jnp.concatenate([cos, cos, cos, cos], axis=-1),
            jnp.concatenate([-sin, sin, -sin, sin], axis=-1))


def _layer_weights(l, w_in, w_pool, s_pool, g_q, w_uq, g_kv, w_ukv, g_sg, b_sg, w_s, b_s):
    d = D_MODEL
    off_cq = (N_BRANCH + 1) * d
    off_ckv = off_cq + Q_LORA
    off_kr = off_ckv + KV_LORA
    off_sg = off_kr + QK_ROPE
    wi = w_in[l]
    w_kr = wi[:, off_kr:off_sg]
    w_kr_sw = _swap_halves(w_kr)
    w_in_r = jnp.concatenate(
        [wi[:, :off_kr], w_kr, w_kr, w_kr_sw, w_kr_sw, wi[:, off_sg:]], axis=1).astype(BF16)
    w_q_rope = w_uq[l][:, :, QK_NOPE:]
    return dict(
        w_in=w_in_r,
        w_pool=w_pool[l].astype(BF16),
        s_pool=s_pool[l].reshape(1, d),
        g_q=g_q[l].reshape(1, Q_LORA),
        w_qn=w_uq[l][:, :, :QK_NOPE].reshape(Q_LORA, MLA_HEADS * QK_NOPE).astype(BF16),
        w_qr=w_q_rope.reshape(Q_LORA, MLA_HEADS * QK_ROPE).astype(BF16),
        w_qrs=_swap_halves(w_q_rope).reshape(Q_LORA, MLA_HEADS * QK_ROPE).astype(BF16),
        w_ukt=jnp.transpose(w_ukv[l][:, :, :QK_NOPE], (1, 2, 0)).astype(BF16),
        g_kv=g_kv[l].reshape(1, KV_LORA),
        g_sg=g_sg[l].reshape(1, d),
        b_sg=b_sg[l].reshape(1, d),
        w_s=w_s[l],
        b_s=jnp.repeat(b_s[l], SG_GW, axis=1),
        w_uv=jnp.transpose(w_ukv[l][:, :, QK_NOPE:], (1, 0, 2)).astype(BF16),
    )


def kernel(x, c, pos, w_ada, b_ada, w_in, w_pool, s_pool, g_q, w_uq, g_kv, w_ukv, g_sg, b_sg, w_s, b_s, w_branch, w_o, ln_t_g, ln_t_b, w_up, conv_w, conv_b, w_down, ln_f_g, ln_f_b):
    depth = w_ada.shape[0]
    d = D_MODEL
    alpha = (2 * depth) ** 0.25
    assert x.shape[-1] == d and x.shape[1] % TM_OUT == 0
    cos, sin = _rope_tables(pos)
    mod = _modulation(c, w_ada, b_ada)
    for l in range(depth):
        w = _layer_weights(l, w_in, w_pool, s_pool, g_q, w_uq, g_kv, w_ukv, g_sg, b_sg, w_s, b_s)
        w_uv = w.pop("w_uv")
        gates, ya, q, k, yc = _mixer_in(x, mod[l], cos, sin, w)
        yb = _attention(q, k, w_uv)
        x = _mixer_out(x, mod[l], gates, ya, yb, yc, w_branch[l].astype(BF16),
                       w_o[l].astype(BF16), ln_t_g[l].reshape(1, d), ln_t_b[l].reshape(1, d), alpha)
        x = _conv_ffn(x, mod[l], w_up[l].astype(BF16), conv_w[l], conv_b[l].reshape(1, 2 * D_FF),
                      w_down[l].astype(BF16), ln_f_g[l].reshape(1, d), ln_f_b[l].reshape(1, d), alpha)
    return x
```

```python
import functools

import jax
import jax.numpy as jnp
from jax import lax
from jax.experimental import pallas as pl
from jax.experimental.pallas import tpu as pltpu

F32 = jnp.float32
BF16 = jnp.bfloat16

D_MODEL = 1024
CHUNK = 64
N_BRANCH = 3
POOL_WINDOWS = (2, 4, 8, 16)
POOL_GW = D_MODEL // len(POOL_WINDOWS)
POOL_HALO = 16
MLA_HEADS = 8
QK_NOPE = 128
QK_ROPE = 64
V_DIM = 128
Q_LORA = D_MODEL // 4
KV_LORA = D_MODEL // 8
ROPE_THETA = 10000.0
SG_BLOCK = 128
SG_GROUPS = 8
SG_GW = D_MODEL // SG_GROUPS
D_FF = 2816
CONV_W = 3
CONV_HALO = 8
LN_EPS = 1e-5
RMS_EPS = 1e-6
LANES = 128
ATT_SCALE = (QK_NOPE + QK_ROPE) ** -0.5 * 1.4426950408889634
MASK_VALUE = -1e30

C_GATE = 0
C_POOL = C_GATE + N_BRANCH * D_MODEL
C_LAT = C_POOL + D_MODEL
LAT_W = Q_LORA + KV_LORA + 4 * QK_ROPE
C_UV = C_LAT + LAT_W
IN_W = C_UV + 2 * D_MODEL

VMEM_LIMIT_BYTES = 60 * 1024 * 1024

TS_IN = 256
TQ_ATT = 256
TM_OUT = 512
TS_FFN = 256
FFN_CHUNK = 256


def _layer_norm(x):
    mu = jnp.mean(x, axis=-1, keepdims=True)
    xc = x - mu
    var = jnp.mean(xc * xc, axis=-1, keepdims=True)
    return xc * lax.rsqrt(var + LN_EPS)


def _rms_norm(x, g):
    return x * lax.rsqrt(jnp.mean(x * x, axis=-1, keepdims=True) + RMS_EPS) * g


def _dot(a, b):
    return jnp.dot(a, b, preferred_element_type=F32)


def _const_spec(shape):
    nd = len(shape)
    return pl.BlockSpec(shape, lambda *_: (0,) * nd, pipeline_mode=pl.Buffered(1))


def _mod_kernel(c_ref, w_ref, b_ref, o_ref):
    c = c_ref[...]
    c_act = c * jax.nn.sigmoid(c)
    o_ref[...] = jnp.dot(c_act, w_ref[...], preferred_element_type=F32,
                         precision=lax.Precision.HIGHEST) + b_ref[...]


def _modulation(c, w_ada, b_ada):
    depth, d, six_d = w_ada.shape
    nb = c.shape[0]
    out = pl.pallas_call(
        _mod_kernel,
        grid=(depth, six_d // d),
        in_specs=[
            pl.BlockSpec((nb, d), lambda l, j: (0, 0)),
            pl.BlockSpec((None, d, d), lambda l, j: (l, 0, j)),
            pl.BlockSpec((None, 1, d), lambda l, j: (l, 0, j)),
        ],
        out_specs=pl.BlockSpec((None, nb, d), lambda l, j: (l, 0, j)),
        out_shape=jax.ShapeDtypeStruct((depth, nb, six_d), F32),
        name="adaln_mod",
    )(c, w_ada, b_ada.reshape(depth, 1, six_d))
    return out.reshape(depth, nb, 6, d)


def _mixer_in_kernel(x_ref, mod_ref, cos_ref, sin_ref, w_in_ref, w_pool_ref, s_pool_ref,
                     g_q_ref, w_qn_ref, w_qr_ref, w_qrs_ref, g_kv_ref,
                     g_sg_ref, b_sg_ref, w_s_ref, b_s_ref,
                     gates_ref, ya_ref, q_ref, k_ref, yc_ref, abuf):
    ts = x_ref.shape[0]
    s_idx = pl.program_id(1)

    h = _layer_norm(x_ref[...]) * (1.0 + mod_ref[1:2, :]) + mod_ref[0:1, :]
    hb = h.astype(BF16)

    gates_ref[...] = jax.nn.sigmoid(
        _dot(hb, w_in_ref[:, C_GATE:C_GATE + N_BRANCH * D_MODEL])).astype(BF16)

    @pl.when(s_idx == 0)
    def _():
        abuf[0:POOL_HALO, :] = jnp.zeros((POOL_HALO, D_MODEL), F32)

    abuf[POOL_HALO:POOL_HALO + ts, :] = _dot(hb, w_in_ref[:, C_POOL:C_POOL + D_MODEL])
    pos = lax.broadcasted_iota(jnp.int32, (ts, 1), 0) + s_idx * ts
    for g, w in enumerate(POOL_WINDOWS):
        cols = slice(g * POOL_GW, (g + 1) * POOL_GW)
        cur = abuf[POOL_HALO:POOL_HALO + ts, cols]
        win = cur
        for k in range(1, w):
            win = win + abuf[POOL_HALO - k:POOL_HALO - k + ts, cols]
        cnt = jnp.minimum(pos + 1, w).astype(F32)
        pooled = win / cnt - cur
        mixed = _dot(pooled.astype(BF16), w_pool_ref[g])
        ya_ref[:, cols] = (mixed * s_pool_ref[:, cols]).astype(BF16)
    abuf[0:POOL_HALO, :] = abuf[ts:ts + POOL_HALO, :]

    lat = _dot(hb, w_in_ref[:, C_LAT:C_LAT + LAT_W])
    cos = cos_ref[...]
    sin = sin_ref[...]
    cqn = _rms_norm(lat[:, 0:Q_LORA], g_q_ref[...]).astype(BF16)
    ckvn = _rms_norm(lat[:, Q_LORA:Q_LORA + KV_LORA], g_kv_ref[...])
    o_kr = Q_LORA + KV_LORA
    kr = lat[:, o_kr:o_kr + 2 * QK_ROPE] * cos + lat[:, o_kr + 2 * QK_ROPE:o_kr + 4 * QK_ROPE] * sin
    k_ref[:, 0:KV_LORA] = ckvn.astype(BF16)
    k_ref[:, KV_LORA:KV_LORA + 2 * QK_ROPE] = kr.astype(BF16)

    q_ref[:, 0:MLA_HEADS * KV_LORA] = (_dot(cqn, w_qn_ref[...]) * ATT_SCALE).astype(BF16)
    reps = MLA_HEADS * QK_ROPE // cos.shape[1]
    cos_t = jnp.concatenate([cos] * reps, axis=1)
    sin_t = jnp.concatenate([sin] * reps, axis=1)
    q_rope = (_dot(cqn, w_qr_ref[...]) * cos_t + _dot(cqn, w_qrs_ref[...]) * sin_t) * ATT_SCALE
    q_ref[:, MLA_HEADS * QK_NOPE:] = q_rope.astype(BF16)

    uv = _dot(hb, w_in_ref[:, C_UV:C_UV + 2 * D_MODEL])
    uv = 0.5 * uv * (1.0 + lax.erf(uv * (2.0 ** -0.5)))
    u = uv[:, :D_MODEL]
    v = (_layer_norm(uv[:, D_MODEL:]) * g_sg_ref[...] + b_sg_ref[...]).astype(BF16)
    row = lax.broadcasted_iota(jnp.int32, (SG_BLOCK, SG_BLOCK), 0)
    col = lax.broadcasted_iota(jnp.int32, (SG_BLOCK, SG_BLOCK), 1)
    n_blk = ts // SG_BLOCK
    for g in range(SG_GROUPS):
        cols = slice(g * SG_GW, (g + 1) * SG_GW)
        w_tril = jnp.where(col <= row, w_s_ref[g], 0.0).astype(BF16)
        v_cat = jnp.concatenate(
            [v[n * SG_BLOCK:(n + 1) * SG_BLOCK, cols] for n in range(n_blk)], axis=1)
        z = _dot(w_tril, v_cat)
        for n in range(n_blk):
            rows = slice(n * SG_BLOCK, (n + 1) * SG_BLOCK)
            zn = z[:, n * SG_GW:(n + 1) * SG_GW] + b_s_ref[:, cols]
            yc_ref[rows, cols] = (u[rows, cols] * zn).astype(BF16)


def _mixer_in(x, mod_l, cos, sin, w):
    nb, seq, d = x.shape
    ts = TS_IN
    tile = lambda width: pl.BlockSpec((None, ts, width), lambda b, s: (b, s, 0))
    consts = [w["w_in"], w["w_pool"], w["s_pool"], w["g_q"], w["w_qn"], w["w_qr"], w["w_qrs"],
              w["g_kv"], w["g_sg"], w["b_sg"], w["w_s"], w["b_s"]]
    q_w = MLA_HEADS * (QK_NOPE + QK_ROPE)
    k_w = KV_LORA + 2 * QK_ROPE
    return pl.pallas_call(
        _mixer_in_kernel,
        grid=(nb, seq // ts),
        in_specs=[tile(d),
                  pl.BlockSpec((None, 6, d), lambda b, s: (b, 0, 0)),
                  tile(cos.shape[-1]), tile(sin.shape[-1])]
                 + [_const_spec(a.shape) for a in consts],
        out_specs=[tile(N_BRANCH * d), tile(d), tile(q_w), tile(k_w), tile(d)],
        out_shape=[jax.ShapeDtypeStruct((nb, seq, N_BRANCH * d), BF16),
                   jax.ShapeDtypeStruct((nb, seq, d), BF16),
                   jax.ShapeDtypeStruct((nb, seq, q_w), BF16),
                   jax.ShapeDtypeStruct((nb, seq, k_w), BF16),
                   jax.ShapeDtypeStruct((nb, seq, d), BF16)],
        scratch_shapes=[pltpu.VMEM((ts + POOL_HALO, d), F32)],
        compiler_params=pltpu.CompilerParams(
            dimension_semantics=("arbitrary", "arbitrary"),
            vmem_limit_bytes=VMEM_LIMIT_BYTES),
        name="mixer_in",
    )(x, mod_l, cos, sin, *consts)


def _attention_kernel(q_ref, kt_ref, k_ref, o_ref, q_scr, m_scr, l_scr, acc_scr):
    tq = q_ref.shape[0]
    tk = kt_ref.shape[2]
    rows = MLA_HEADS * tq
    n_fold = tk // LANES
    i = pl.program_id(1)

    lane = lax.broadcasted_iota(jnp.int32, (tq, 2 * QK_ROPE), 1)
    for hd in range(MLA_HEADS):
        pair = q_ref[:, MLA_HEADS * QK_NOPE + (hd // 2) * 2 * QK_ROPE:
                     MLA_HEADS * QK_NOPE + (hd // 2 + 1) * 2 * QK_ROPE]
        keep = (lane < QK_ROPE) if hd % 2 == 0 else (lane >= QK_ROPE)
        q_scr[hd * tq:(hd + 1) * tq, 0:KV_LORA] = q_ref[:, hd * QK_NOPE:(hd + 1) * QK_NOPE]
        q_scr[hd * tq:(hd + 1) * tq, KV_LORA:] = jnp.where(keep, pair, jnp.zeros_like(pair))

    def scores(j):
        return _dot(q_scr[...], kt_ref[j])

    shift = CHUNK.bit_length() - 1
    q_chunk = lax.broadcasted_iota(jnp.int32, (tq, tk), 0) >> shift
    k_chunk = lax.broadcasted_iota(jnp.int32, (tq, tk), 1) >> shift
    bias = jnp.where(k_chunk <= q_chunk, 0.0, MASK_VALUE).astype(F32)

    def diag_scores():
        s = scores(i).reshape(MLA_HEADS, tq, tk) + bias[None]
        return s.reshape(rows, tk)

    def lane_tiles(s):
        return [s[:, c * LANES:(c + 1) * LANES] for c in range(n_fold)]

    def fold_max(s):
        return functools.reduce(jnp.maximum, lane_tiles(s))

    m_scr[...] = jnp.full((rows, LANES), MASK_VALUE, F32)

    def max_body(j, carry):
        m_scr[...] = jnp.maximum(m_scr[...], fold_max(scores(j)))
        return carry

    lax.fori_loop(0, i, max_body, 0)
    m_part = jnp.maximum(m_scr[...], fold_max(diag_scores()))
    m_scr[...] = jnp.broadcast_to(jnp.max(m_part, axis=1, keepdims=True), (rows, LANES))

    l_scr[...] = jnp.zeros((rows, LANES), F32)
    acc_scr[...] = jnp.zeros((rows, KV_LORA), F32)

    def accumulate(j, s):
        m = m_scr[...]
        p_tiles = [jnp.exp2(t - m) for t in lane_tiles(s)]
        l_scr[...] += functools.reduce(jnp.add, p_tiles)
        p = jnp.concatenate(p_tiles, axis=1).astype(BF16)
        v = k_ref[pl.ds(pl.multiple_of(j * tk, tk), tk), 0:KV_LORA]
        acc_scr[...] += _dot(p, v)

    def acc_body(j, carry):
        accumulate(j, scores(j))
        return carry

    lax.fori_loop(0, i, acc_body, 0)
    accumulate(i, diag_scores())

    o_lat = (acc_scr[...] / jnp.sum(l_scr[...], axis=1, keepdims=True)).astype(BF16)
    for hd in range(MLA_HEADS):
        o_ref[:, hd * KV_LORA:(hd + 1) * KV_LORA] = o_lat[hd * tq:(hd + 1) * tq, :]


def _attention(q, k):
    nb, seq, q_w = q.shape
    k_w = k.shape[-1]
    tq = TQ_ATT
    rows = MLA_HEADS * tq
    kt = jnp.swapaxes(k.reshape(nb, seq // tq, tq, k_w), 2, 3)
    return pl.pallas_call(
        _attention_kernel,
        grid=(nb, seq // tq),
        in_specs=[pl.BlockSpec((None, tq, q_w), lambda b, i: (b, i, 0)),
                  pl.BlockSpec((None, seq // tq, k_w, tq), lambda b, i: (b, 0, 0, 0)),
                  pl.BlockSpec((None, seq, k_w), lambda b, i: (b, 0, 0))],
        out_specs=pl.BlockSpec((None, tq, MLA_HEADS * KV_LORA), lambda b, i: (b, i, 0)),
        out_shape=jax.ShapeDtypeStruct((nb, seq, MLA_HEADS * KV_LORA), BF16),
        scratch_shapes=[pltpu.VMEM((rows, k_w), BF16),
                        pltpu.VMEM((rows, LANES), F32),
                        pltpu.VMEM((rows, LANES), F32),
                        pltpu.VMEM((rows, KV_LORA), F32)],
        compiler_params=pltpu.CompilerParams(
            dimension_semantics=("arbitrary", "arbitrary"),
            vmem_limit_bytes=VMEM_LIMIT_BYTES),
        name="latent_attention",
    )(q, kt, k)


def _mixer_out_kernel(x_ref, mod_ref, gates_ref, ya_ref, yb_ref, yc_ref, w_br_ref, w_o_ref,
                      ln_g_ref, ln_b_ref, o_ref, *, alpha):
    d = x_ref.shape[1]
    merged = None
    for br, y_ref in enumerate((ya_ref, yb_ref, yc_ref)):
        y = _dot(y_ref[...], w_br_ref[br])
        term = gates_ref[:, br * d:(br + 1) * d].astype(F32) * y
        merged = term if merged is None else merged + term
    y = _dot(merged.astype(BF16), w_o_ref[...])
    r = alpha * x_ref[...] + mod_ref[2:3, :] * y
    o_ref[...] = _layer_norm(r) * ln_g_ref[...] + ln_b_ref[...]


def _mixer_out(x, mod_l, gates, ya, yb, yc, w_br, w_o, ln_g, ln_b, alpha):
    nb, seq, d = x.shape
    tm = TM_OUT
    tile = lambda width: pl.BlockSpec((None, tm, width), lambda b, s: (b, s, 0))
    return pl.pallas_call(
        functools.partial(_mixer_out_kernel, alpha=alpha),
        grid=(nb, seq // tm),
        in_specs=[tile(d),
                  pl.BlockSpec((None, 6, d), lambda b, s: (b, 0, 0)),
                  tile(N_BRANCH * d), tile(d), tile(d), tile(d),
                  _const_spec(w_br.shape), _const_spec(w_o.shape),
                  _const_spec(ln_g.shape), _const_spec(ln_b.shape)],
        out_specs=tile(d),
        out_shape=jax.ShapeDtypeStruct((nb, seq, d), F32),
        compiler_params=pltpu.CompilerParams(
            dimension_semantics=("arbitrary", "arbitrary"),
            vmem_limit_bytes=VMEM_LIMIT_BYTES),
        name="mixer_out",
    )(x, mod_l, gates, ya, yb, yc, w_br, w_o, ln_g, ln_b)


def _conv_ffn_kernel(x_ref, mod_ref, w_up_ref, cw_ref, cb_ref, w_down_ref, ln_g_ref, ln_b_ref,
                     o_ref, zbuf, *, alpha):
    ts = x_ref.shape[0]
    s_idx = pl.program_id(1)
    x = x_ref[...]
    h = (_layer_norm(x) * (1.0 + mod_ref[4:5, :]) + mod_ref[3:4, :]).astype(BF16)

    @pl.when(s_idx == 0)
    def _():
        zbuf[0:CONV_HALO, :] = jnp.zeros((CONV_HALO, 2 * D_FF), F32)

    n_chunks = D_FF // FFN_CHUNK

    def chunk_cols(c):
        return (slice(c * FFN_CHUNK, (c + 1) * FFN_CHUNK),
                slice(D_FF + c * FFN_CHUNK, D_FF + (c + 1) * FFN_CHUNK))

    def up(c):
        return [_dot(h, w_up_ref[:, cols]) for cols in chunk_cols(c)]

    def conv(z, cols):
        ext = jnp.concatenate([zbuf[:, cols], z], axis=0)
        zc = cb_ref[:, cols] + cw_ref[CONV_W - 1:CONV_W, cols] * z
        for k in range(1, CONV_W):
            shifted = pltpu.roll(ext, k, 0)[CONV_HALO:CONV_HALO + ts]
            zc = zc + cw_ref[CONV_W - 1 - k:CONV_W - k, cols] * shifted
        zbuf[:, cols] = z[ts - CONV_HALO:ts]
        return zc

    def down(c, act):
        return _dot(act, w_down_ref[c * FFN_CHUNK:(c + 1) * FFN_CHUNK, :])

    y = jnp.zeros((ts, D_MODEL), F32)
    z_next = up(0)
    act_prev = None
    for c in range(n_chunks):
        z_cur = z_next
        if c + 1 < n_chunks:
            z_next = up(c + 1)
        if act_prev is not None:
            y = y + down(c - 1, act_prev)
        val, gate = [conv(z, cols) for z, cols in zip(z_cur, chunk_cols(c))]
        act_prev = (gate * jax.nn.sigmoid(gate) * val).astype(BF16)
    y = y + down(n_chunks - 1, act_prev)
    r = alpha * x + mod_ref[5:6, :] * y
    o_ref[...] = _layer_norm(r) * ln_g_ref[...] + ln_b_ref[...]


def _conv_ffn(x, mod_l, w_up, conv_w, conv_b, w_down, ln_g, ln_b, alpha):
    nb, seq, d = x.shape
    ts = TS_FFN
    tile = pl.BlockSpec((None, ts, d), lambda b, s: (b, s, 0))
    consts = [w_up, conv_w, conv_b, w_down, ln_g, ln_b]
    return pl.pallas_call(
        functools.partial(_conv_ffn_kernel, alpha=alpha),
        grid=(nb, seq // ts),
        in_specs=[tile, pl.BlockSpec((None, 6, d), lambda b, s: (b, 0, 0))]
                 + [_const_spec(a.shape) for a in consts],
        out_specs=tile,
        out_shape=jax.ShapeDtypeStruct((nb, seq, d), F32),
        scratch_shapes=[pltpu.VMEM((CONV_HALO, 2 * D_FF), F32)],
        compiler_params=pltpu.CompilerParams(
            dimension_semantics=("arbitrary", "arbitrary"),
            vmem_limit_bytes=VMEM_LIMIT_BYTES),
        name="conv_ffn",
    )(x, mod_l, w_up, conv_w, conv_b, w_down, ln_g, ln_b)


def _swap_halves(w):
    half = w.shape[-1] // 2
    return jnp.concatenate([w[..., half:], w[..., :half]], axis=-1)


def _rope_tables(pos):
    freqs = ROPE_THETA ** (-jnp.arange(0, QK_ROPE, 2, dtype=F32) / QK_ROPE)
    ang = pos.astype(F32)[..., None] * freqs
    cos, sin = jnp.cos(ang), jnp.sin(ang)
    return (jnp.concatenate([cos, cos, cos, cos], axis=-1),
            jnp.concatenate([-sin, sin, -sin, sin], axis=-1))


def _fold_kernel(a_ref, b_ref, o_ref):
    o_ref[...] = jnp.dot(a_ref[...], b_ref[...], preferred_element_type=F32,
                         precision=lax.Precision.HIGHEST).astype(o_ref.dtype)


def _fold_heads(a, b):
    n, m, kk = a.shape
    nn = b.shape[2]
    return pl.pallas_call(
        _fold_kernel,
        grid=(n,),
        in_specs=[pl.BlockSpec((None, m, kk), lambda h: (h, 0, 0)),
                  pl.BlockSpec((None, kk, nn), lambda h: (h, 0, 0))],
        out_specs=pl.BlockSpec((None, m, nn), lambda h: (h, 0, 0)),
        out_shape=jax.ShapeDtypeStruct((n, m, nn), BF16),
        name="fold_weights",
    )(a, b)


def _absorbed_weights(w_uq, w_ukv, w_branch):
    depth = w_uq.shape[0]
    h, d = MLA_HEADS, D_MODEL
    uq_nope = jnp.transpose(w_uq[..., :QK_NOPE], (0, 2, 1, 3)).reshape(depth * h, Q_LORA, QK_NOPE)
    uk_t = jnp.transpose(w_ukv[..., :QK_NOPE], (0, 2, 3, 1)).reshape(depth * h, QK_NOPE, KV_LORA)
    w_q = _fold_heads(uq_nope, uk_t).reshape(depth, h, Q_LORA, KV_LORA)
    w_q = jnp.transpose(w_q, (0, 2, 1, 3)).reshape(depth, Q_LORA, h * KV_LORA)
    uv = jnp.transpose(w_ukv[..., QK_NOPE:], (0, 2, 1, 3)).reshape(depth * h, KV_LORA, V_DIM)
    w_b = _fold_heads(uv, w_branch[:, 1].reshape(depth * h, V_DIM, d)).reshape(depth, h * KV_LORA, d)
    return w_q, w_b


def _layer_weights(l, w_in, w_pool, s_pool, g_q, w_uq, g_kv, g_sg, b_sg, w_s, b_s, w_q_abs):
    d = D_MODEL
    off_cq = (N_BRANCH + 1) * d
    off_ckv = off_cq + Q_LORA
    off_kr = off_ckv + KV_LORA
    off_sg = off_kr + QK_ROPE
    wi = w_in[l]
    w_kr = wi[:, off_kr:off_sg]
    w_kr_sw = _swap_halves(w_kr)
    w_in_r = jnp.concatenate(
        [wi[:, :off_kr], w_kr, w_kr, w_kr_sw, w_kr_sw, wi[:, off_sg:]], axis=1).astype(BF16)
    w_q_rope = w_uq[l][:, :, QK_NOPE:]
    return dict(
        w_in=w_in_r,
        w_pool=w_pool[l].astype(BF16),
        s_pool=s_pool[l].reshape(1, d),
        g_q=g_q[l].reshape(1, Q_LORA),
        w_qn=w_q_abs[l],
        w_qr=w_q_rope.reshape(Q_LORA, MLA_HEADS * QK_ROPE).astype(BF16),
        w_qrs=_swap_halves(w_q_rope).reshape(Q_LORA, MLA_HEADS * QK_ROPE).astype(BF16),
        g_kv=g_kv[l].reshape(1, KV_LORA),
        g_sg=g_sg[l].reshape(1, d),
        b_sg=b_sg[l].reshape(1, d),
        w_s=w_s[l],
        b_s=jnp.repeat(b_s[l], SG_GW, axis=1),
    )


def kernel(x, c, pos, w_ada, b_ada, w_in, w_pool, s_pool, g_q, w_uq, g_kv, w_ukv, g_sg, b_sg, w_s, b_s, w_branch, w_o, ln_t_g, ln_t_b, w_up, conv_w, conv_b, w_down, ln_f_g, ln_f_b):
    depth = w_ada.shape[0]
    d = D_MODEL
    alpha = (2 * depth) ** 0.25
    assert x.shape[-1] == d and x.shape[1] % TM_OUT == 0
    cos, sin = _rope_tables(pos)
    mod = _modulation(c, w_ada, b_ada)
    w_q_abs, w_b_abs = _absorbed_weights(w_uq, w_ukv, w_branch)
    for l in range(depth):
        w = _layer_weights(l, w_in, w_pool, s_pool, g_q, w_uq, g_kv, g_sg, b_sg, w_s, b_s, w_q_abs)
        gates, ya, q, k, yc = _mixer_in(x, mod[l], cos, sin, w)
        yb = _attention(q, k)
        w_br = jnp.stack([w_branch[l, 0].astype(BF16), w_b_abs[l], w_branch[l, 2].astype(BF16)])
        x = _mixer_out(x, mod[l], gates, ya, yb, yc, w_br,
                       w_o[l].astype(BF16), ln_t_g[l].reshape(1, d), ln_t_b[l].reshape(1, d), alpha)
        x = _conv_ffn(x, mod[l], w_up[l].astype(BF16), conv_w[l], conv_b[l].reshape(1, 2 * D_FF),
                      w_down[l].astype(BF16), ln_f_g[l].reshape(1, d), ln_f_b[l].reshape(1, d), alpha)
    return x
```

```python
import functools

import jax
import jax.numpy as jnp
from jax import lax
from jax.experimental import pallas as pl
from jax.experimental.pallas import tpu as pltpu

F32 = jnp.float32
BF16 = jnp.bfloat16

D_MODEL = 1024
CHUNK = 64
N_BRANCH = 3
POOL_WINDOWS = (2, 4, 8, 16)
POOL_GW = D_MODEL // len(POOL_WINDOWS)
POOL_HALO = 16
MLA_HEADS = 8
QK_NOPE = 128
QK_ROPE = 64
V_DIM = 128
Q_LORA = D_MODEL // 4
KV_LORA = D_MODEL // 8
ROPE_THETA = 10000.0
SG_BLOCK = 128
SG_GROUPS = 8
SG_GW = D_MODEL // SG_GROUPS
D_FF = 2816
CONV_W = 3
CONV_HALO = 8
LN_EPS = 1e-5
RMS_EPS = 1e-6
LANES = 128
ATT_SCALE = (QK_NOPE + QK_ROPE) ** -0.5 * 1.4426950408889634
MASK_VALUE = -1e30

C_GATE = 0
C_POOL = C_GATE + N_BRANCH * D_MODEL
C_LAT = C_POOL + D_MODEL
LAT_W = Q_LORA + KV_LORA + 4 * QK_ROPE
C_UV = C_LAT + LAT_W
IN_W = C_UV + 2 * D_MODEL

VMEM_LIMIT_BYTES = 60 * 1024 * 1024

ROW_SPLIT = 2
TS_IN = 512
GATE_CHUNK = 512
UV_CHUNK = 512
TQ_ATT = 256
TM_OUT = 512
TS_FFN = 512
FFN_CHUNK = 512


def _layer_norm(x):
    mu = jnp.mean(x, axis=-1, keepdims=True)
    xc = x - mu
    var = jnp.mean(xc * xc, axis=-1, keepdims=True)
    return xc * lax.rsqrt(var + LN_EPS)


def _rms_norm(x, g):
    return x * lax.rsqrt(jnp.mean(x * x, axis=-1, keepdims=True) + RMS_EPS) * g


def _dot(a, b):
    return jnp.dot(a, b, preferred_element_type=F32)


def _const_spec(shape):
    nd = len(shape)
    return pl.BlockSpec(shape, lambda *_: (0,) * nd, pipeline_mode=pl.Buffered(1))


def _mod_kernel(c_ref, w_ref, b_ref, o_ref):
    c = c_ref[...]
    c_act = c * jax.nn.sigmoid(c)
    o_ref[...] = jnp.dot(c_act, w_ref[...], preferred_element_type=F32,
                         precision=lax.Precision.HIGHEST) + b_ref[...]


def _modulation(c, w_ada, b_ada):
    depth, d, six_d = w_ada.shape
    nb = c.shape[0]
    out = pl.pallas_call(
        _mod_kernel,
        grid=(depth, six_d // d),
        in_specs=[
            pl.BlockSpec((nb, d), lambda l, j: (0, 0)),
            pl.BlockSpec((None, d, d), lambda l, j: (l, 0, j)),
            pl.BlockSpec((None, 1, d), lambda l, j: (l, 0, j)),
        ],
        out_specs=pl.BlockSpec((None, nb, d), lambda l, j: (l, 0, j)),
        out_shape=jax.ShapeDtypeStruct((depth, nb, six_d), F32),
        name="adaln_mod",
    )(c, w_ada, b_ada.reshape(depth, 1, six_d))
    return out.reshape(depth, nb, 6, d)


def _mixer_in_kernel(x_ref, mod_ref, cos_ref, sin_ref, w_in_ref, w_pool_ref, s_pool_ref,
                     g_q_ref, w_qn_ref, w_qr_ref, w_qrs_ref, g_kv_ref,
                     g_sg_ref, b_sg_ref, w_s_ref, b_s_ref,
                     gates_ref, ya_ref, q_ref, k_ref, yc_ref, halo):
    ts = x_ref.shape[0]
    hr = ts // ROW_SPLIT
    s_idx = pl.program_id(1)
    d = D_MODEL
    n_gate = N_BRANCH * d // GATE_CHUNK
    n_blk = hr // SG_BLOCK

    @pl.when(s_idx == 0)
    def _():
        halo[...] = jnp.zeros((POOL_HALO, d), F32)

    def prologue(r0):
        h = _layer_norm(x_ref[r0:r0 + hr, :]) * (1.0 + mod_ref[1:2, :]) + mod_ref[0:1, :]
        return h.astype(BF16)

    def gelu(t):
        return 0.5 * t * (1.0 + lax.erf(t * (2.0 ** -0.5)))

    def first_proj(hb):
        return (_dot(hb, w_in_ref[:, C_POOL:C_POOL + d]), _dot(hb, w_in_ref[:, C_LAT:C_LAT + LAT_W]))

    def body(r0, hb, a, lat):
        rows = slice(r0, r0 + hr)

        def proj(start, width):
            return _dot(hb, w_in_ref[:, start:start + width])

        pos = lax.broadcasted_iota(jnp.int32, (hr, 1), 0) + (s_idx * ts + r0)
        pooled = []
        for g, w in enumerate(POOL_WINDOWS):
            cols = slice(g * POOL_GW, (g + 1) * POOL_GW)
            cur = a[:, cols]
            win = jnp.concatenate([halo[:, cols], cur], axis=0)
            span = 1
            while span < w:
                win = win + pltpu.roll(win, span, 0)
                span *= 2
            cnt = jnp.minimum(pos + 1, w).astype(F32)
            pooled.append((win[POOL_HALO:POOL_HALO + hr] / cnt - cur).astype(BF16))
        halo[...] = a[hr - POOL_HALO:hr, :]

        uv_v = [proj(C_UV + d + c * UV_CHUNK, UV_CHUNK) for c in range(d // UV_CHUNK)]

        cos = cos_ref[rows, :]
        sin = sin_ref[rows, :]
        cqn = _rms_norm(lat[:, 0:Q_LORA], g_q_ref[...]).astype(BF16)
        ckvn = _rms_norm(lat[:, Q_LORA:Q_LORA + KV_LORA], g_kv_ref[...])
        o_kr = Q_LORA + KV_LORA
        kr = (lat[:, o_kr:o_kr + 2 * QK_ROPE] * cos
              + lat[:, o_kr + 2 * QK_ROPE:o_kr + 4 * QK_ROPE] * sin)
        k_ref[rows, 0:KV_LORA] = ckvn.astype(BF16)
        k_ref[rows, KV_LORA:KV_LORA + 2 * QK_ROPE] = kr.astype(BF16)

        mixed = [_dot(pooled[g], w_pool_ref[g]) for g in range(len(POOL_WINDOWS))]
        q_lat = _dot(cqn, w_qn_ref[...])
        q_r = _dot(cqn, w_qr_ref[...])
        q_rs = _dot(cqn, w_qrs_ref[...])

        v = jnp.concatenate([gelu(t) for t in uv_v], axis=1)
        v = (_layer_norm(v) * g_sg_ref[...] + b_sg_ref[...]).astype(BF16)

        uv_u = [proj(C_UV + c * UV_CHUNK, UV_CHUNK) for c in range(d // UV_CHUNK)]

        for g in range(len(POOL_WINDOWS)):
            cols = slice(g * POOL_GW, (g + 1) * POOL_GW)
            ya_ref[rows, cols] = (mixed[g] * s_pool_ref[:, cols]).astype(BF16)
        q_ref[rows, 0:MLA_HEADS * KV_LORA] = (q_lat * ATT_SCALE).astype(BF16)
        reps = MLA_HEADS * QK_ROPE // cos.shape[1]
        cos_t = jnp.concatenate([cos] * reps, axis=1)
        sin_t = jnp.concatenate([sin] * reps, axis=1)
        q_ref[rows, MLA_HEADS * QK_NOPE:] = ((q_r * cos_t + q_rs * sin_t) * ATT_SCALE).astype(BF16)

        gate_z = [proj(C_GATE + c * GATE_CHUNK, GATE_CHUNK) for c in range(2)]

        u = jnp.concatenate([gelu(t) for t in uv_u], axis=1)

        tri_r = lax.broadcasted_iota(jnp.int32, (SG_BLOCK, SG_BLOCK), 0)
        tri_c = lax.broadcasted_iota(jnp.int32, (SG_BLOCK, SG_BLOCK), 1)
        z = []
        for g in range(SG_GROUPS):
            cols = slice(g * SG_GW, (g + 1) * SG_GW)
            w_tril = jnp.where(tri_c <= tri_r, w_s_ref[g], 0.0).astype(BF16)
            v_cat = jnp.concatenate(
                [v[n * SG_BLOCK:(n + 1) * SG_BLOCK, cols] for n in range(n_blk)], axis=1)
            z.append(_dot(w_tril, v_cat))

        def store_gates(c, t):
            gates_ref[rows, c * GATE_CHUNK:(c + 1) * GATE_CHUNK] = jax.nn.sigmoid(t).astype(BF16)

        for c in range(2, n_gate, 2):
            nxt = [proj(C_GATE + cc * GATE_CHUNK, GATE_CHUNK) for cc in range(c, min(c + 2, n_gate))]
            for k, t in enumerate(gate_z):
                store_gates(c - 2 + k, t)
            if c == 2:
                for g in range(SG_GROUPS):
                    cols = slice(g * SG_GW, (g + 1) * SG_GW)
                    for n in range(n_blk):
                        zn = z[g][:, n * SG_GW:(n + 1) * SG_GW] + b_s_ref[:, cols]
                        blk = slice(n * SG_BLOCK, (n + 1) * SG_BLOCK)
                        yc_ref[r0 + n * SG_BLOCK:r0 + (n + 1) * SG_BLOCK, cols] = (
                            u[blk, cols] * zn).astype(BF16)
            gate_z = nxt

        def tail():
            for k, t in enumerate(gate_z):
                store_gates(n_gate - len(gate_z) + k, t)
        return tail

    hb = prologue(0)
    first = first_proj(hb)
    tail = None
    for k in range(ROW_SPLIT):
        hb_cur, first_cur = hb, first
        if k + 1 < ROW_SPLIT:
            hb = prologue((k + 1) * hr)
        if tail is not None:
            tail()
        tail = body(k * hr, hb_cur, *first_cur)
        if k + 1 < ROW_SPLIT:
            first = first_proj(hb)
    tail()


def _mixer_in(x, mod_l, cos, sin, w):
    nb, seq, d = x.shape
    ts = TS_IN
    tile = lambda width: pl.BlockSpec((None, ts, width), lambda b, s: (b, s, 0))
    consts = [w["w_in"], w["w_pool"], w["s_pool"], w["g_q"], w["w_qn"], w["w_qr"], w["w_qrs"],
              w["g_kv"], w["g_sg"], w["b_sg"], w["w_s"], w["b_s"]]
    q_w = MLA_HEADS * (QK_NOPE + QK_ROPE)
    k_w = KV_LORA + 2 * QK_ROPE
    return pl.pallas_call(
        _mixer_in_kernel,
        grid=(nb, seq // ts),
        in_specs=[tile(d),
                  pl.BlockSpec((None, 6, d), lambda b, s: (b, 0, 0)),
                  tile(cos.shape[-1]), tile(sin.shape[-1])]
                 + [_const_spec(a.shape) for a in consts],
        out_specs=[tile(N_BRANCH * d), tile(d), tile(q_w), tile(k_w), tile(d)],
        out_shape=[jax.ShapeDtypeStruct((nb, seq, N_BRANCH * d), BF16),
                   jax.ShapeDtypeStruct((nb, seq, d), BF16),
                   jax.ShapeDtypeStruct((nb, seq, q_w), BF16),
                   jax.ShapeDtypeStruct((nb, seq, k_w), BF16),
                   jax.ShapeDtypeStruct((nb, seq, d), BF16)],
        scratch_shapes=[pltpu.VMEM((POOL_HALO, d), F32)],
        compiler_params=pltpu.CompilerParams(
            dimension_semantics=("arbitrary", "arbitrary"),
            vmem_limit_bytes=VMEM_LIMIT_BYTES),
        name="mixer_in",
    )(x, mod_l, cos, sin, *consts)


def _attention_kernel(q_ref, kt_ref, k_ref, o_ref, q_scr, m_scr, l_scr, acc_scr, s_scr):
    tq = q_ref.shape[0]
    tk = kt_ref.shape[2]
    rows = MLA_HEADS * tq
    n_fold = tk // LANES
    i = pl.program_id(1)

    lane = lax.broadcasted_iota(jnp.int32, (tq, 2 * QK_ROPE), 1)
    for hd in range(MLA_HEADS):
        pair = q_ref[:, MLA_HEADS * QK_NOPE + (hd // 2) * 2 * QK_ROPE:
                     MLA_HEADS * QK_NOPE + (hd // 2 + 1) * 2 * QK_ROPE]
        keep = (lane < QK_ROPE) if hd % 2 == 0 else (lane >= QK_ROPE)
        q_scr[hd * tq:(hd + 1) * tq, 0:KV_LORA] = q_ref[:, hd * QK_NOPE:(hd + 1) * QK_NOPE]
        q_scr[hd * tq:(hd + 1) * tq, KV_LORA:] = jnp.where(keep, pair, jnp.zeros_like(pair))

    def scores(j):
        return _dot(q_scr[...], kt_ref[j])

    shift = CHUNK.bit_length() - 1
    q_chunk = lax.broadcasted_iota(jnp.int32, (tq, tk), 0) >> shift
    k_chunk = lax.broadcasted_iota(jnp.int32, (tq, tk), 1) >> shift
    bias = jnp.where(k_chunk <= q_chunk, 0.0, MASK_VALUE).astype(F32)

    def diag_scores():
        s = scores(i).reshape(MLA_HEADS, tq, tk) + bias[None]
        return s.reshape(rows, tk)

    def lane_tiles(s):
        return [s[:, c * LANES:(c + 1) * LANES] for c in range(n_fold)]

    def fold_max(s):
        return functools.reduce(jnp.maximum, lane_tiles(s))

    m_scr[...] = jnp.full((rows, LANES), MASK_VALUE, F32)

    def max_body(j, carry):
        m_scr[...] = jnp.maximum(m_scr[...], fold_max(scores(j)))
        return carry

    lax.fori_loop(0, i, max_body, 0)
    m_part = jnp.maximum(m_scr[...], fold_max(diag_scores()))
    m_scr[...] = jnp.broadcast_to(jnp.max(m_part, axis=1, keepdims=True), (rows, LANES))

    l_scr[...] = jnp.zeros((rows, LANES), F32)
    acc_scr[...] = jnp.zeros((rows, KV_LORA), F32)

    def accumulate(j, s):
        m = m_scr[...]
        p_tiles = [jnp.exp2(t - m) for t in lane_tiles(s)]
        l_scr[...] += functools.reduce(jnp.add, p_tiles)
        p = jnp.concatenate(p_tiles, axis=1).astype(BF16)
        v = k_ref[pl.ds(pl.multiple_of(j * tk, tk), tk), 0:KV_LORA]
        acc_scr[...] += _dot(p, v)

    s_scr[...] = diag_scores()

    def acc_body(j, carry):
        s_next = scores(j)
        accumulate(jnp.where(j == 0, i, j - 1), s_scr[...])
        s_scr[...] = s_next
        return carry

    lax.fori_loop(0, i, acc_body, 0)
    accumulate(jnp.maximum(i - 1, 0), s_scr[...])

    o_lat = (acc_scr[...] / jnp.sum(l_scr[...], axis=1, keepdims=True)).astype(BF16)
    for hd in range(MLA_HEADS):
        o_ref[:, hd * KV_LORA:(hd + 1) * KV_LORA] = o_lat[hd * tq:(hd + 1) * tq, :]


def _attention(q, k):
    nb, seq, q_w = q.shape
    k_w = k.shape[-1]
    tq = TQ_ATT
    rows = MLA_HEADS * tq
    kt = jnp.swapaxes(k.reshape(nb, seq // tq, tq, k_w), 2, 3)
    return pl.pallas_call(
        _attention_kernel,
        grid=(nb, seq // tq),
        in_specs=[pl.BlockSpec((None, tq, q_w), lambda b, i: (b, i, 0)),
                  pl.BlockSpec((None, seq // tq, k_w, tq), lambda b, i: (b, 0, 0, 0)),
                  pl.BlockSpec((None, seq, k_w), lambda b, i: (b, 0, 0))],
        out_specs=pl.BlockSpec((None, tq, MLA_HEADS * KV_LORA), lambda b, i: (b, i, 0)),
        out_shape=jax.ShapeDtypeStruct((nb, seq, MLA_HEADS * KV_LORA), BF16),
        scratch_shapes=[pltpu.VMEM((rows, k_w), BF16),
                        pltpu.VMEM((rows, LANES), F32),
                        pltpu.VMEM((rows, LANES), F32),
                        pltpu.VMEM((rows, KV_LORA), F32),
                        pltpu.VMEM((rows, tq), F32)],
        compiler_params=pltpu.CompilerParams(
            dimension_semantics=("arbitrary", "arbitrary"),
            vmem_limit_bytes=VMEM_LIMIT_BYTES),
        name="latent_attention",
    )(q, kt, k)


def _mixer_out_kernel(x_ref, mod_ref, gates_ref, ya_ref, yb_ref, yc_ref, w_br_ref, w_o_ref,
                      ln_g_ref, ln_b_ref, o_ref, *, alpha):
    tm, d = x_ref.shape
    hr = tm // ROW_SPLIT
    branches = (ya_ref, yb_ref, yc_ref)

    def branch_dots(r0):
        return [_dot(y_ref[r0:r0 + hr, :], w_br_ref[br]) for br, y_ref in enumerate(branches)]

    def merge(r0, ys):
        merged = None
        for br, y in enumerate(ys):
            term = gates_ref[r0:r0 + hr, br * d:(br + 1) * d].astype(F32) * y
            merged = term if merged is None else merged + term
        return merged.astype(BF16)

    def epilogue(r0, y):
        r = alpha * x_ref[r0:r0 + hr, :] + mod_ref[2:3, :] * y
        o_ref[r0:r0 + hr, :] = _layer_norm(r) * ln_g_ref[...] + ln_b_ref[...]

    ys = branch_dots(0)
    pending = None
    for k in range(ROW_SPLIT):
        ys_cur = ys
        if k + 1 < ROW_SPLIT:
            ys = branch_dots((k + 1) * hr)
        merged = merge(k * hr, ys_cur)
        y = _dot(merged, w_o_ref[...])
        if pending is not None:
            epilogue(*pending)
        pending = (k * hr, y)
    epilogue(*pending)


def _mixer_out(x, mod_l, gates, ya, yb, yc, w_br, w_o, ln_g, ln_b, alpha):
    nb, seq, d = x.shape
    tm = TM_OUT
    tile = lambda width: pl.BlockSpec((None, tm, width), lambda b, s: (b, s, 0))
    return pl.pallas_call(
        functools.partial(_mixer_out_kernel, alpha=alpha),
        grid=(nb, seq // tm),
        in_specs=[tile(d),
                  pl.BlockSpec((None, 6, d), lambda b, s: (b, 0, 0)),
                  tile(N_BRANCH * d), tile(d), tile(d), tile(d),
                  _const_spec(w_br.shape), _const_spec(w_o.shape),
                  _const_spec(ln_g.shape), _const_spec(ln_b.shape)],
        out_specs=tile(d),
        out_shape=jax.ShapeDtypeStruct((nb, seq, d), F32),
        compiler_params=pltpu.CompilerParams(
            dimension_semantics=("arbitrary", "arbitrary"),
            vmem_limit_bytes=VMEM_LIMIT_BYTES),
        name="mixer_out",
    )(x, mod_l, gates, ya, yb, yc, w_br, w_o, ln_g, ln_b)


def _conv_ffn_kernel(x_ref, mod_ref, w_up_ref, cw_ref, cb_ref, w_down_ref, ln_g_ref, ln_b_ref,
                     o_ref, zhalo, *, alpha):
    ts = x_ref.shape[0]
    hr = ts // ROW_SPLIT
    s_idx = pl.program_id(1)
    bounds = list(range(0, D_FF, FFN_CHUNK)) + [D_FF]
    n_chunks = len(bounds) - 1

    @pl.when(s_idx == 0)
    def _():
        zhalo[...] = jnp.zeros((CONV_HALO, 2 * D_FF), F32)

    def prologue(r0):
        h = _layer_norm(x_ref[r0:r0 + hr, :]) * (1.0 + mod_ref[4:5, :]) + mod_ref[3:4, :]
        return h.astype(BF16)

    def chunk_cols(c):
        return (slice(bounds[c], bounds[c + 1]), slice(D_FF + bounds[c], D_FF + bounds[c + 1]))

    def up(hb, c):
        return [_dot(hb, w_up_ref[:, cols]) for cols in chunk_cols(c)]

    def conv(z, cols):
        ext = jnp.concatenate([zhalo[:, cols], z], axis=0)
        zc = cb_ref[:, cols] + cw_ref[CONV_W - 1:CONV_W, cols] * z
        for k in range(1, CONV_W):
            shifted = pltpu.roll(ext, k, 0)[CONV_HALO:CONV_HALO + hr]
            zc = zc + cw_ref[CONV_W - 1 - k:CONV_W - k, cols] * shifted
        zhalo[:, cols] = z[hr - CONV_HALO:hr]
        return zc

    def down(c, act):
        return _dot(act, w_down_ref[bounds[c]:bounds[c + 1], :])

    def epilogue(r0, y):
        r = alpha * x_ref[r0:r0 + hr, :] + mod_ref[5:6, :] * y
        o_ref[r0:r0 + hr, :] = _layer_norm(r) * ln_g_ref[...] + ln_b_ref[...]

    hb = prologue(0)
    z_next = up(hb, 0)
    pending = None
    for k in range(ROW_SPLIT):
        hb_cur = hb
        if k + 1 < ROW_SPLIT:
            hb = prologue((k + 1) * hr)
        y = jnp.zeros((hr, D_MODEL), F32)
        act_prev = None
        for c in range(n_chunks):
            z_cur = z_next
            if c + 1 < n_chunks:
                z_next = up(hb_cur, c + 1)
            elif k + 1 < ROW_SPLIT:
                z_next = up(hb, 0)
            if act_prev is not None:
                y = y + down(c - 1, act_prev)
            if c == 0 and pending is not None:
                epilogue(*pending)
            val, gate = [conv(z, cols) for z, cols in zip(z_cur, chunk_cols(c))]
            act_prev = (gate * jax.nn.sigmoid(gate) * val).astype(BF16)
        y = y + down(n_chunks - 1, act_prev)
        pending = (k * hr, y)
    epilogue(*pending)


def _conv_ffn(x, mod_l, w_up, conv_w, conv_b, w_down, ln_g, ln_b, alpha):
    nb, seq, d = x.shape
    ts = TS_FFN
    tile = pl.BlockSpec((None, ts, d), lambda b, s: (b, s, 0))
    consts = [w_up, conv_w, conv_b, w_down, ln_g, ln_b]
    return pl.pallas_call(
        functools.partial(_conv_ffn_kernel, alpha=alpha),
        grid=(nb, seq // ts),
        in_specs=[tile, pl.BlockSpec((None, 6, d), lambda b, s: (b, 0, 0))]
                 + [_const_spec(a.shape) for a in consts],
        out_specs=tile,
        out_shape=jax.ShapeDtypeStruct((nb, seq, d), F32),
        scratch_shapes=[pltpu.VMEM((CONV_HALO, 2 * D_FF), F32)],
        compiler_params=pltpu.CompilerParams(
            dimension_semantics=("arbitrary", "arbitrary"),
            vmem_limit_bytes=VMEM_LIMIT_BYTES),
        name="conv_ffn",
    )(x, mod_l, w_up, conv_w, conv_b, w_down, ln_g, ln_b)


def _swap_halves(w):
    half = w.shape[-1] // 2
    return jnp.concatenate([w[..., half:], w[..., :half]], axis=-1)


def _rope_tables(pos):
    freqs = ROPE_THETA ** (-jnp.arange(0, QK_ROPE, 2, dtype=F32) / QK_ROPE)
    ang = pos.astype(F32)[..., None] * freqs
    cos, sin = jnp.cos(ang), jnp.sin(ang)
    return (jnp.concatenate([cos, cos, cos, cos], axis=-1),
            jnp.concatenate([-sin, sin, -sin, sin], axis=-1))


def _fold_kernel(a_ref, b_ref, o_ref):
    o_ref[...] = jnp.dot(a_ref[...], b_ref[...], preferred_element_type=F32,
                         precision=lax.Precision.HIGHEST).astype(o_ref.dtype)


def _fold_heads(a, b):
    n, m, kk = a.shape
    nn = b.shape[2]
    return pl.pallas_call(
        _fold_kernel,
        grid=(n,),
        in_specs=[pl.BlockSpec((None, m, kk), lambda h: (h, 0, 0)),
                  pl.BlockSpec((None, kk, nn), lambda h: (h, 0, 0))],
        out_specs=pl.BlockSpec((None, m, nn), lambda h: (h, 0, 0)),
        out_shape=jax.ShapeDtypeStruct((n, m, nn), BF16),
        name="fold_weights",
    )(a, b)


def _absorbed_weights(w_uq, w_ukv, w_branch):
    depth = w_uq.shape[0]
    h, d = MLA_HEADS, D_MODEL
    uq_nope = jnp.transpose(w_uq[..., :QK_NOPE], (0, 2, 1, 3)).reshape(depth * h, Q_LORA, QK_NOPE)
    uk_t = jnp.transpose(w_ukv[..., :QK_NOPE], (0, 2, 3, 1)).reshape(depth * h, QK_NOPE, KV_LORA)
    w_q = _fold_heads(uq_nope, uk_t).reshape(depth, h, Q_LORA, KV_LORA)
    w_q = jnp.transpose(w_q, (0, 2, 1, 3)).reshape(depth, Q_LORA, h * KV_LORA)
    uv = jnp.transpose(w_ukv[..., QK_NOPE:], (0, 2, 1, 3)).reshape(depth * h, KV_LORA, V_DIM)
    w_b = _fold_heads(uv, w_branch[:, 1].reshape(depth * h, V_DIM, d)).reshape(depth, h * KV_LORA, d)
    return w_q, w_b


def _layer_weights(l, w_in, w_pool, s_pool, g_q, w_uq, g_kv, g_sg, b_sg, w_s, b_s, w_q_abs):
    d = D_MODEL
    off_cq = (N_BRANCH + 1) * d
    off_ckv = off_cq + Q_LORA
    off_kr = off_ckv + KV_LORA
    off_sg = off_kr + QK_ROPE
    wi = w_in[l]
    w_kr = wi[:, off_kr:off_sg]
    w_kr_sw = _swap_halves(w_kr)
    w_in_r = jnp.concatenate(
        [wi[:, :off_kr], w_kr, w_kr, w_kr_sw, w_kr_sw, wi[:, off_sg:]], axis=1).astype(BF16)
    w_q_rope = w_uq[l][:, :, QK_NOPE:]
    return dict(
        w_in=w_in_r,
        w_pool=w_pool[l].astype(BF16),
        s_pool=s_pool[l].reshape(1, d),
        g_q=g_q[l].reshape(1, Q_LORA),
        w_qn=w_q_abs[l],
        w_qr=w_q_rope.reshape(Q_LORA, MLA_HEADS * QK_ROPE).astype(BF16),
        w_qrs=_swap_halves(w_q_rope).reshape(Q_LORA, MLA_HEADS * QK_ROPE).astype(BF16),
        g_kv=g_kv[l].reshape(1, KV_LORA),
        g_sg=g_sg[l].reshape(1, d),
        b_sg=b_sg[l].reshape(1, d),
        w_s=w_s[l],
        b_s=jnp.repeat(b_s[l], SG_GW, axis=1),
    )


def kernel(x, c, pos, w_ada, b_ada, w_in, w_pool, s_pool, g_q, w_uq, g_kv, w_ukv, g_sg, b_sg, w_s, b_s, w_branch, w_o, ln_t_g, ln_t_b, w_up, conv_w, conv_b, w_down, ln_f_g, ln_f_b):
    depth = w_ada.shape[0]
    d = D_MODEL
    alpha = (2 * depth) ** 0.25
    assert x.shape[-1] == d and x.shape[1] % TM_OUT == 0
    cos, sin = _rope_tables(pos)
    mod = _modulation(c, w_ada, b_ada)
    w_q_abs, w_b_abs = _absorbed_weights(w_uq, w_ukv, w_branch)
    for l in range(depth):
        w = _layer_weights(l, w_in, w_pool, s_pool, g_q, w_uq, g_kv, g_sg, b_sg, w_s, b_s, w_q_abs)
        gates, ya, q, k, yc = _mixer_in(x, mod[l], cos, sin, w)
        yb = _attention(q, k)
        w_br = jnp.stack([w_branch[l, 0].astype(BF16), w_b_abs[l], w_branch[l, 2].astype(BF16)])
        x = _mixer_out(x, mod[l], gates, ya, yb, yc, w_br,
                       w_o[l].astype(BF16), ln_t_g[l].reshape(1, d), ln_t_b[l].reshape(1, d), alpha)
        x = _conv_ffn(x, mod[l], w_up[l].astype(BF16), conv_w[l], conv_b[l].reshape(1, 2 * D_FF),
                      w_down[l].astype(BF16), ln_f_g[l].reshape(1, d), ln_f_b[l].reshape(1, d), alpha)
    return x
```

```python
import functools

import jax
import jax.numpy as jnp
from jax import lax
from jax.experimental import pallas as pl
from jax.experimental.pallas import tpu as pltpu

F32 = jnp.float32
BF16 = jnp.bfloat16

D_MODEL = 1024
CHUNK = 64
N_BRANCH = 3
POOL_WINDOWS = (2, 4, 8, 16)
POOL_GW = D_MODEL // len(POOL_WINDOWS)
POOL_HALO = 16
MLA_HEADS = 8
QK_NOPE = 128
QK_ROPE = 64
V_DIM = 128
Q_LORA = D_MODEL // 4
KV_LORA = D_MODEL // 8
ROPE_THETA = 10000.0
SG_BLOCK = 128
SG_GROUPS = 8
SG_GW = D_MODEL // SG_GROUPS
D_FF = 2816
CONV_W = 3
CONV_HALO = 8
LN_EPS = 1e-5
RMS_EPS = 1e-6
LANES = 128
ATT_SCALE = (QK_NOPE + QK_ROPE) ** -0.5 * 1.4426950408889634
MASK_VALUE = -1e30

C_GATE = 0
C_POOL = C_GATE + N_BRANCH * D_MODEL
C_LAT = C_POOL + D_MODEL
LAT_W = Q_LORA + KV_LORA + 4 * QK_ROPE
C_UV = C_LAT + LAT_W
IN_W = C_UV + 2 * D_MODEL

VMEM_LIMIT_BYTES = 60 * 1024 * 1024

ROW_SPLIT = 2
TS_IN = 512
GATE_CHUNK = 512
UV_CHUNK = 512
TQ_ATT = 256
TM_OUT = 512
TS_FFN = 512
FFN_CHUNK = 512


def _layer_norm(x):
    mu = jnp.mean(x, axis=-1, keepdims=True)
    xc = x - mu
    var = jnp.mean(xc * xc, axis=-1, keepdims=True)
    return xc * lax.rsqrt(var + LN_EPS)


def _rms_norm(x, g):
    return x * lax.rsqrt(jnp.mean(x * x, axis=-1, keepdims=True) + RMS_EPS) * g


def _dot(a, b):
    return jnp.dot(a, b, preferred_element_type=F32)


def _const_spec(shape):
    nd = len(shape)
    return pl.BlockSpec(shape, lambda *_: (0,) * nd, pipeline_mode=pl.Buffered(1))


def _mod_kernel(c_ref, w_ref, b_ref, o_ref):
    c = c_ref[...]
    c_act = c * jax.nn.sigmoid(c)
    o_ref[...] = jnp.dot(c_act, w_ref[...], preferred_element_type=F32,
                         precision=lax.Precision.HIGHEST) + b_ref[...]


def _modulation(c, w_ada, b_ada):
    depth, d, six_d = w_ada.shape
    nb = c.shape[0]
    out = pl.pallas_call(
        _mod_kernel,
        grid=(depth, six_d // d),
        in_specs=[
            pl.BlockSpec((nb, d), lambda l, j: (0, 0)),
            pl.BlockSpec((None, d, d), lambda l, j: (l, 0, j)),
            pl.BlockSpec((None, 1, d), lambda l, j: (l, 0, j)),
        ],
        out_specs=pl.BlockSpec((None, nb, d), lambda l, j: (l, 0, j)),
        out_shape=jax.ShapeDtypeStruct((depth, nb, six_d), F32),
        name="adaln_mod",
    )(c, w_ada, b_ada.reshape(depth, 1, six_d))
    return out.reshape(depth, nb, 6, d)


def _mixer_in_kernel(x_ref, mod_ref, cos_ref, sin_ref, w_in_ref, w_pool_ref, s_pool_ref,
                     g_q_ref, w_qn_ref, w_qr_ref, w_qrs_ref, g_kv_ref,
                     g_sg_ref, b_sg_ref, w_s_ref, b_s_ref,
                     gates_ref, ya_ref, q_ref, k_ref, kt_ref, yc_ref, halo):
    ts = x_ref.shape[0]
    hr = ts // ROW_SPLIT
    s_idx = pl.program_id(1)
    d = D_MODEL
    n_gate = N_BRANCH * d // GATE_CHUNK
    n_blk = hr // SG_BLOCK

    @pl.when(s_idx == 0)
    def _():
        halo[...] = jnp.zeros((POOL_HALO, d), F32)

    def prologue(r0):
        h = _layer_norm(x_ref[r0:r0 + hr, :]) * (1.0 + mod_ref[1:2, :]) + mod_ref[0:1, :]
        return h.astype(BF16)

    def gelu(t):
        return 0.5 * t * (1.0 + lax.erf(t * (2.0 ** -0.5)))

    def first_proj(hb):
        return (_dot(hb, w_in_ref[:, C_POOL:C_POOL + d]), _dot(hb, w_in_ref[:, C_LAT:C_LAT + LAT_W]))

    def body(r0, hb, a, lat):
        rows = slice(r0, r0 + hr)

        def proj(start, width):
            return _dot(hb, w_in_ref[:, start:start + width])

        pos = lax.broadcasted_iota(jnp.int32, (hr, 1), 0) + (s_idx * ts + r0)
        pooled = []
        for g, w in enumerate(POOL_WINDOWS):
            cols = slice(g * POOL_GW, (g + 1) * POOL_GW)
            cur = a[:, cols]
            win = jnp.concatenate([halo[:, cols], cur], axis=0)
            span = 1
            while span < w:
                win = win + pltpu.roll(win, span, 0)
                span *= 2
            cnt = jnp.minimum(pos + 1, w).astype(F32)
            pooled.append((win[POOL_HALO:POOL_HALO + hr] / cnt - cur).astype(BF16))
        halo[...] = a[hr - POOL_HALO:hr, :]

        uv_v = [proj(C_UV + d + c * UV_CHUNK, UV_CHUNK) for c in range(d // UV_CHUNK)]

        cos = cos_ref[rows, :]
        sin = sin_ref[rows, :]
        cqn = _rms_norm(lat[:, 0:Q_LORA], g_q_ref[...]).astype(BF16)
        ckvn = _rms_norm(lat[:, Q_LORA:Q_LORA + KV_LORA], g_kv_ref[...])
        o_kr = Q_LORA + KV_LORA
        kr = (lat[:, o_kr:o_kr + 2 * QK_ROPE] * cos
              + lat[:, o_kr + 2 * QK_ROPE:o_kr + 4 * QK_ROPE] * sin)
        k_tile = jnp.concatenate([ckvn.astype(BF16), kr.astype(BF16)], axis=1)
        k_ref[rows, :] = k_tile
        for n in range(hr // TQ_ATT):
            kt_ref[r0 // TQ_ATT + n] = k_tile[n * TQ_ATT:(n + 1) * TQ_ATT, :].T

        mixed = [_dot(pooled[g], w_pool_ref[g]) for g in range(len(POOL_WINDOWS))]
        q_lat = _dot(cqn, w_qn_ref[...])
        q_r = _dot(cqn, w_qr_ref[...])
        q_rs = _dot(cqn, w_qrs_ref[...])

        v = jnp.concatenate([gelu(t) for t in uv_v], axis=1)
        v = (_layer_norm(v) * g_sg_ref[...] + b_sg_ref[...]).astype(BF16)

        uv_u = [proj(C_UV + c * UV_CHUNK, UV_CHUNK) for c in range(d // UV_CHUNK)]

        for g in range(len(POOL_WINDOWS)):
            cols = slice(g * POOL_GW, (g + 1) * POOL_GW)
            ya_ref[rows, cols] = (mixed[g] * s_pool_ref[:, cols]).astype(BF16)
        q_ref[rows, 0:MLA_HEADS * KV_LORA] = (q_lat * ATT_SCALE).astype(BF16)
        reps = MLA_HEADS * QK_ROPE // cos.shape[1]
        cos_t = jnp.concatenate([cos] * reps, axis=1)
        sin_t = jnp.concatenate([sin] * reps, axis=1)
        q_ref[rows, MLA_HEADS * QK_NOPE:] = ((q_r * cos_t + q_rs * sin_t) * ATT_SCALE).astype(BF16)

        gate_z = [proj(C_GATE + c * GATE_CHUNK, GATE_CHUNK) for c in range(2)]

        u = jnp.concatenate([gelu(t) for t in uv_u], axis=1)

        tri_r = lax.broadcasted_iota(jnp.int32, (SG_BLOCK, SG_BLOCK), 0)
        tri_c = lax.broadcasted_iota(jnp.int32, (SG_BLOCK, SG_BLOCK), 1)
        z = []
        for g in range(SG_GROUPS):
            cols = slice(g * SG_GW, (g + 1) * SG_GW)
            w_tril = jnp.where(tri_c <= tri_r, w_s_ref[g], 0.0).astype(BF16)
            v_cat = jnp.concatenate(
                [v[n * SG_BLOCK:(n + 1) * SG_BLOCK, cols] for n in range(n_blk)], axis=1)
            z.append(_dot(w_tril, v_cat))

        def store_gates(c, t):
            gates_ref[rows, c * GATE_CHUNK:(c + 1) * GATE_CHUNK] = jax.nn.sigmoid(t).astype(BF16)

        for c in range(2, n_gate, 2):
            nxt = [proj(C_GATE + cc * GATE_CHUNK, GATE_CHUNK) for cc in range(c, min(c + 2, n_gate))]
            for k, t in enumerate(gate_z):
                store_gates(c - 2 + k, t)
            if c == 2:
                for g in range(SG_GROUPS):
                    cols = slice(g * SG_GW, (g + 1) * SG_GW)
                    for n in range(n_blk):
                        zn = z[g][:, n * SG_GW:(n + 1) * SG_GW] + b_s_ref[:, cols]
                        blk = slice(n * SG_BLOCK, (n + 1) * SG_BLOCK)
                        yc_ref[r0 + n * SG_BLOCK:r0 + (n + 1) * SG_BLOCK, cols] = (
                            u[blk, cols] * zn).astype(BF16)
            gate_z = nxt

        def tail():
            for k, t in enumerate(gate_z):
                store_gates(n_gate - len(gate_z) + k, t)
        return tail

    hb = prologue(0)
    first = first_proj(hb)
    tail = None
    for k in range(ROW_SPLIT):
        hb_cur, first_cur = hb, first
        if k + 1 < ROW_SPLIT:
            hb = prologue((k + 1) * hr)
        if tail is not None:
            tail()
        tail = body(k * hr, hb_cur, *first_cur)
        if k + 1 < ROW_SPLIT:
            first = first_proj(hb)
    tail()


def _mixer_in(x, mod_l, cos, sin, w):
    nb, seq, d = x.shape
    ts = TS_IN
    tile = lambda width: pl.BlockSpec((None, ts, width), lambda b, s: (b, s, 0))
    consts = [w["w_in"], w["w_pool"], w["s_pool"], w["g_q"], w["w_qn"], w["w_qr"], w["w_qrs"],
              w["g_kv"], w["g_sg"], w["b_sg"], w["w_s"], w["b_s"]]
    q_w = MLA_HEADS * (QK_NOPE + QK_ROPE)
    k_w = KV_LORA + 2 * QK_ROPE
    return pl.pallas_call(
        _mixer_in_kernel,
        grid=(nb, seq // ts),
        in_specs=[tile(d),
                  pl.BlockSpec((None, 6, d), lambda b, s: (b, 0, 0)),
                  tile(cos.shape[-1]), tile(sin.shape[-1])]
                 + [_const_spec(a.shape) for a in consts],
        out_specs=[tile(N_BRANCH * d), tile(d), tile(q_w), tile(k_w),
                   pl.BlockSpec((None, ts // TQ_ATT, k_w, TQ_ATT), lambda b, s: (b, s, 0, 0)),
                   tile(d)],
        out_shape=[jax.ShapeDtypeStruct((nb, seq, N_BRANCH * d), BF16),
                   jax.ShapeDtypeStruct((nb, seq, d), BF16),
                   jax.ShapeDtypeStruct((nb, seq, q_w), BF16),
                   jax.ShapeDtypeStruct((nb, seq, k_w), BF16),
                   jax.ShapeDtypeStruct((nb, seq // TQ_ATT, k_w, TQ_ATT), BF16),
                   jax.ShapeDtypeStruct((nb, seq, d), BF16)],
        scratch_shapes=[pltpu.VMEM((POOL_HALO, d), F32)],
        compiler_params=pltpu.CompilerParams(
            dimension_semantics=("arbitrary", "arbitrary"),
            vmem_limit_bytes=VMEM_LIMIT_BYTES),
        name="mixer_in",
    )(x, mod_l, cos, sin, *consts)


def _attention_kernel(q_ref, kt_ref, k_ref, o_ref, q_scr, m_scr, l_scr, acc_scr, s_scr):
    tq = q_ref.shape[0]
    tk = kt_ref.shape[2]
    rows = MLA_HEADS * tq
    n_fold = tk // LANES
    i = pl.program_id(1)

    lane = lax.broadcasted_iota(jnp.int32, (tq, 2 * QK_ROPE), 1)
    for hd in range(MLA_HEADS):
        pair = q_ref[:, MLA_HEADS * QK_NOPE + (hd // 2) * 2 * QK_ROPE:
                     MLA_HEADS * QK_NOPE + (hd // 2 + 1) * 2 * QK_ROPE]
        keep = (lane < QK_ROPE) if hd % 2 == 0 else (lane >= QK_ROPE)
        q_scr[hd * tq:(hd + 1) * tq, 0:KV_LORA] = q_ref[:, hd * QK_NOPE:(hd + 1) * QK_NOPE]
        q_scr[hd * tq:(hd + 1) * tq, KV_LORA:] = jnp.where(keep, pair, jnp.zeros_like(pair))

    def scores(j):
        return _dot(q_scr[...], kt_ref[j])

    shift = CHUNK.bit_length() - 1
    q_chunk = lax.broadcasted_iota(jnp.int32, (tq, tk), 0) >> shift
    k_chunk = lax.broadcasted_iota(jnp.int32, (tq, tk), 1) >> shift
    bias = jnp.where(k_chunk <= q_chunk, 0.0, MASK_VALUE).astype(F32)

    def diag_scores():
        s = scores(i).reshape(MLA_HEADS, tq, tk) + bias[None]
        return s.reshape(rows, tk)

    def lane_tiles(s):
        return [s[:, c * LANES:(c + 1) * LANES] for c in range(n_fold)]

    def row_max(s):
        part = functools.reduce(jnp.maximum, lane_tiles(s))
        return jnp.broadcast_to(jnp.max(part, axis=1, keepdims=True), (rows, LANES))

    def value_block(j):
        return k_ref[pl.ds(pl.multiple_of(j * tk, tk), tk), 0:KV_LORA]

    s0 = diag_scores()
    m0 = row_max(s0)
    p_tiles = [jnp.exp2(t - m0) for t in lane_tiles(s0)]
    m_scr[...] = m0
    l_scr[...] = functools.reduce(jnp.add, p_tiles)
    acc_scr[...] = _dot(jnp.concatenate(p_tiles, axis=1).astype(BF16), value_block(i))

    def update(j, s):
        m_old = m_scr[...]
        m_new = jnp.maximum(m_old, row_max(s))
        alpha = jnp.exp2(m_old - m_new)
        p_tiles = [jnp.exp2(t - m_new) for t in lane_tiles(s)]
        l_scr[...] = alpha * l_scr[...] + functools.reduce(jnp.add, p_tiles)
        p = jnp.concatenate(p_tiles, axis=1).astype(BF16)
        acc_scr[...] = alpha * acc_scr[...] + _dot(p, value_block(j))
        m_scr[...] = m_new

    @pl.when(i > 0)
    def _():
        s_scr[...] = scores(0)

        def body(j, carry):
            s_next = scores(j + 1)
            update(j, s_scr[...])
            s_scr[...] = s_next
            return carry

        lax.fori_loop(0, i - 1, body, 0)
        update(i - 1, s_scr[...])

    o_lat = (acc_scr[...] / jnp.sum(l_scr[...], axis=1, keepdims=True)).astype(BF16)
    for hd in range(MLA_HEADS):
        o_ref[:, hd * KV_LORA:(hd + 1) * KV_LORA] = o_lat[hd * tq:(hd + 1) * tq, :]


def _attention(q, k, kt):
    nb, seq, q_w = q.shape
    k_w = k.shape[-1]
    tq = TQ_ATT
    rows = MLA_HEADS * tq
    return pl.pallas_call(
        _attention_kernel,
        grid=(nb, seq // tq),
        in_specs=[pl.BlockSpec((None, tq, q_w), lambda b, i: (b, i, 0)),
                  pl.BlockSpec((None, seq // tq, k_w, tq), lambda b, i: (b, 0, 0, 0)),
                  pl.BlockSpec((None, seq, k_w), lambda b, i: (b, 0, 0))],
        out_specs=pl.BlockSpec((None, tq, MLA_HEADS * KV_LORA), lambda b, i: (b, i, 0)),
        out_shape=jax.ShapeDtypeStruct((nb, seq, MLA_HEADS * KV_LORA), BF16),
        scratch_shapes=[pltpu.VMEM((rows, k_w), BF16),
                        pltpu.VMEM((rows, LANES), F32),
                        pltpu.VMEM((rows, LANES), F32),
                        pltpu.VMEM((rows, KV_LORA), F32),
                        pltpu.VMEM((rows, tq), F32)],
        compiler_params=pltpu.CompilerParams(
            dimension_semantics=("arbitrary", "arbitrary"),
            vmem_limit_bytes=VMEM_LIMIT_BYTES),
        name="latent_attention",
    )(q, kt, k)


def _mixer_out_kernel(x_ref, mod_ref, gates_ref, ya_ref, yb_ref, yc_ref, w_br_ref, w_o_ref,
                      ln_g_ref, ln_b_ref, o_ref, *, alpha):
    tm, d = x_ref.shape
    hr = tm // ROW_SPLIT
    branches = (ya_ref, yb_ref, yc_ref)

    def branch_dots(r0):
        return [_dot(y_ref[r0:r0 + hr, :], w_br_ref[br]) for br, y_ref in enumerate(branches)]

    def merge(r0, ys):
        merged = None
        for br, y in enumerate(ys):
            term = gates_ref[r0:r0 + hr, br * d:(br + 1) * d].astype(F32) * y
            merged = term if merged is None else merged + term
        return merged.astype(BF16)

    def epilogue(r0, y):
        r = alpha * x_ref[r0:r0 + hr, :] + mod_ref[2:3, :] * y
        o_ref[r0:r0 + hr, :] = _layer_norm(r) * ln_g_ref[...] + ln_b_ref[...]

    ys = branch_dots(0)
    pending = None
    for k in range(ROW_SPLIT):
        ys_cur = ys
        if k + 1 < ROW_SPLIT:
            ys = branch_dots((k + 1) * hr)
        merged = merge(k * hr, ys_cur)
        y = _dot(merged, w_o_ref[...])
        if pending is not None:
            epilogue(*pending)
        pending = (k * hr, y)
    epilogue(*pending)


def _mixer_out(x, mod_l, gates, ya, yb, yc, w_br, w_o, ln_g, ln_b, alpha):
    nb, seq, d = x.shape
    tm = TM_OUT
    tile = lambda width: pl.BlockSpec((None, tm, width), lambda b, s: (b, s, 0))
    return pl.pallas_call(
        functools.partial(_mixer_out_kernel, alpha=alpha),
        grid=(nb, seq // tm),
        in_specs=[tile(d),
                  pl.BlockSpec((None, 6, d), lambda b, s: (b, 0, 0)),
                  tile(N_BRANCH * d), tile(d), tile(d), tile(d),
                  _const_spec(w_br.shape), _const_spec(w_o.shape),
                  _const_spec(ln_g.shape), _const_spec(ln_b.shape)],
        out_specs=tile(d),
        out_shape=jax.ShapeDtypeStruct((nb, seq, d), F32),
        compiler_params=pltpu.CompilerParams(
            dimension_semantics=("arbitrary", "arbitrary"),
            vmem_limit_bytes=VMEM_LIMIT_BYTES),
        name="mixer_out",
    )(x, mod_l, gates, ya, yb, yc, w_br, w_o, ln_g, ln_b)


def _conv_ffn_kernel(x_ref, mod_ref, w_up_ref, cw_ref, cb_ref, w_down_ref, ln_g_ref, ln_b_ref,
                     o_ref, zhalo, *, alpha):
    ts = x_ref.shape[0]
    hr = ts // ROW_SPLIT
    s_idx = pl.program_id(1)
    bounds = list(range(0, D_FF, FFN_CHUNK)) + [D_FF]
    n_chunks = len(bounds) - 1

    @pl.when(s_idx == 0)
    def _():
        zhalo[...] = jnp.zeros((CONV_HALO, 2 * D_FF), F32)

    def prologue(r0):
        h = _layer_norm(x_ref[r0:r0 + hr, :]) * (1.0 + mod_ref[4:5, :]) + mod_ref[3:4, :]
        return h.astype(BF16)

    def chunk_cols(c):
        return (slice(bounds[c], bounds[c + 1]), slice(D_FF + bounds[c], D_FF + bounds[c + 1]))

    def up(hb, c):
        return [_dot(hb, w_up_ref[:, cols]) for cols in chunk_cols(c)]

    def conv(z, cols):
        ext = jnp.concatenate([zhalo[:, cols], z], axis=0)
        zc = cb_ref[:, cols] + cw_ref[CONV_W - 1:CONV_W, cols] * z
        for k in range(1, CONV_W):
            shifted = pltpu.roll(ext, k, 0)[CONV_HALO:CONV_HALO + hr]
            zc = zc + cw_ref[CONV_W - 1 - k:CONV_W - k, cols] * shifted
        zhalo[:, cols] = z[hr - CONV_HALO:hr]
        return zc

    def down(c, act):
        return _dot(act, w_down_ref[bounds[c]:bounds[c + 1], :])

    def epilogue(r0, y):
        r = alpha * x_ref[r0:r0 + hr, :] + mod_ref[5:6, :] * y
        o_ref[r0:r0 + hr, :] = _layer_norm(r) * ln_g_ref[...] + ln_b_ref[...]

    hb = prologue(0)
    z_next = up(hb, 0)
    pending = None
    for k in range(ROW_SPLIT):
        hb_cur = hb
        if k + 1 < ROW_SPLIT:
            hb = prologue((k + 1) * hr)
        y = jnp.zeros((hr, D_MODEL), F32)
        act_prev = None
        for c in range(n_chunks):
            z_cur = z_next
            if c + 1 < n_chunks:
                z_next = up(hb_cur, c + 1)
            elif k + 1 < ROW_SPLIT:
                z_next = up(hb, 0)
            if act_prev is not None:
                y = y + down(c - 1, act_prev)
            if c == 0 and pending is not None:
                epilogue(*pending)
            val, gate = [conv(z, cols) for z, cols in zip(z_cur, chunk_cols(c))]
            act_prev = (gate * jax.nn.sigmoid(gate) * val).astype(BF16)
        y = y + down(n_chunks - 1, act_prev)
        pending = (k * hr, y)
    epilogue(*pending)


def _conv_ffn(x, mod_l, w_up, conv_w, conv_b, w_down, ln_g, ln_b, alpha):
    nb, seq, d = x.shape
    ts = TS_FFN
    tile = pl.BlockSpec((None, ts, d), lambda b, s: (b, s, 0))
    consts = [w_up, conv_w, conv_b, w_down, ln_g, ln_b]
    return pl.pallas_call(
        functools.partial(_conv_ffn_kernel, alpha=alpha),
        grid=(nb, seq // ts),
        in_specs=[tile, pl.BlockSpec((None, 6, d), lambda b, s: (b, 0, 0))]
                 + [_const_spec(a.shape) for a in consts],
        out_specs=tile,
        out_shape=jax.ShapeDtypeStruct((nb, seq, d), F32),
        scratch_shapes=[pltpu.VMEM((CONV_HALO, 2 * D_FF), F32)],
        compiler_params=pltpu.CompilerParams(
            dimension_semantics=("arbitrary", "arbitrary"),
            vmem_limit_bytes=VMEM_LIMIT_BYTES),
        name="conv_ffn",
    )(x, mod_l, w_up, conv_w, conv_b, w_down, ln_g, ln_b)


def _swap_halves(w):
    half = w.shape[-1] // 2
    return jnp.concatenate([w[..., half:], w[..., :half]], axis=-1)


def _rope_tables(pos):
    freqs = ROPE_THETA ** (-jnp.arange(0, QK_ROPE, 2, dtype=F32) / QK_ROPE)
    ang = pos.astype(F32)[..., None] * freqs
    cos, sin = jnp.cos(ang), jnp.sin(ang)
    return (jnp.concatenate([cos, cos, cos, cos], axis=-1),
            jnp.concatenate([-sin, sin, -sin, sin], axis=-1))


def _fold_kernel(a_ref, b_ref, o_ref):
    o_ref[...] = jnp.dot(a_ref[...], b_ref[...], preferred_element_type=F32,
                         precision=lax.Precision.HIGHEST).astype(o_ref.dtype)


def _fold_heads(a, b):
    n, m, kk = a.shape
    nn = b.shape[2]
    return pl.pallas_call(
        _fold_kernel,
        grid=(n,),
        in_specs=[pl.BlockSpec((None, m, kk), lambda h: (h, 0, 0)),
                  pl.BlockSpec((None, kk, nn), lambda h: (h, 0, 0))],
        out_specs=pl.BlockSpec((None, m, nn), lambda h: (h, 0, 0)),
        out_shape=jax.ShapeDtypeStruct((n, m, nn), BF16),
        name="fold_weights",
    )(a, b)


def _absorbed_weights(w_uq, w_ukv, w_branch):
    depth = w_uq.shape[0]
    h, d = MLA_HEADS, D_MODEL
    uq_nope = jnp.transpose(w_uq[..., :QK_NOPE], (0, 2, 1, 3)).reshape(depth * h, Q_LORA, QK_NOPE)
    uk_t = jnp.transpose(w_ukv[..., :QK_NOPE], (0, 2, 3, 1)).reshape(depth * h, QK_NOPE, KV_LORA)
    w_q = _fold_heads(uq_nope, uk_t).reshape(depth, h, Q_LORA, KV_LORA)
    w_q = jnp.transpose(w_q, (0, 2, 1, 3)).reshape(depth, Q_LORA, h * KV_LORA)
    uv = jnp.transpose(w_ukv[..., QK_NOPE:], (0, 2, 1, 3)).reshape(depth * h, KV_LORA, V_DIM)
    w_b = _fold_heads(uv, w_branch[:, 1].reshape(depth * h, V_DIM, d)).reshape(depth, h * KV_LORA, d)
    return w_q, w_b


def _layer_weights(l, w_in, w_pool, s_pool, g_q, w_uq, g_kv, g_sg, b_sg, w_s, b_s, w_q_abs):
    d = D_MODEL
    off_cq = (N_BRANCH + 1) * d
    off_ckv = off_cq + Q_LORA
    off_kr = off_ckv + KV_LORA
    off_sg = off_kr + QK_ROPE
    wi = w_in[l]
    w_kr = wi[:, off_kr:off_sg]
    w_kr_sw = _swap_halves(w_kr)
    w_in_r = jnp.concatenate(
        [wi[:, :off_kr], w_kr, w_kr, w_kr_sw, w_kr_sw, wi[:, off_sg:]], axis=1).astype(BF16)
    w_q_rope = w_uq[l][:, :, QK_NOPE:]
    return dict(
        w_in=w_in_r,
        w_pool=w_pool[l].astype(BF16),
        s_pool=s_pool[l].reshape(1, d),
        g_q=g_q[l].reshape(1, Q_LORA),
        w_qn=w_q_abs[l],
        w_qr=w_q_rope.reshape(Q_LORA, MLA_HEADS * QK_ROPE).astype(BF16),
        w_qrs=_swap_halves(w_q_rope).reshape(Q_LORA, MLA_HEADS * QK_ROPE).astype(BF16),
        g_kv=g_kv[l].reshape(1, KV_LORA),
        g_sg=g_sg[l].reshape(1, d),
        b_sg=b_sg[l].reshape(1, d),
        w_s=w_s[l],
        b_s=jnp.repeat(b_s[l], SG_GW, axis=1),
    )


def kernel(x, c, pos, w_ada, b_ada, w_in, w_pool, s_pool, g_q, w_uq, g_kv, w_ukv, g_sg, b_sg, w_s, b_s, w_branch, w_o, ln_t_g, ln_t_b, w_up, conv_w, conv_b, w_down, ln_f_g, ln_f_b):
    depth = w_ada.shape[0]
    d = D_MODEL
    alpha = (2 * depth) ** 0.25
    assert x.shape[-1] == d and x.shape[1] % TM_OUT == 0
    cos, sin = _rope_tables(pos)
    mod = _modulation(c, w_ada, b_ada)
    w_q_abs, w_b_abs = _absorbed_weights(w_uq, w_ukv, w_branch)
    for l in range(depth):
        w = _layer_weights(l, w_in, w_pool, s_pool, g_q, w_uq, g_kv, g_sg, b_sg, w_s, b_s, w_q_abs)
        gates, ya, q, k, kt, yc = _mixer_in(x, mod[l], cos, sin, w)
        yb = _attention(q, k, kt)
        w_br = jnp.stack([w_branch[l, 0].astype(BF16), w_b_abs[l], w_branch[l, 2].astype(BF16)])
        x = _mixer_out(x, mod[l], gates, ya, yb, yc, w_br,
                       w_o[l].astype(BF16), ln_t_g[l].reshape(1, d), ln_t_b[l].reshape(1, d), alpha)
        x = _conv_ffn(x, mod[l], w_up[l].astype(BF16), conv_w[l], conv_b[l].reshape(1, 2 * D_FF),
                      w_down[l].astype(BF16), ln_f_g[l].reshape(1, d), ln_f_b[l].reshape(1, d), alpha)
    return x
```

```python
import functools

import jax
import jax.numpy as jnp
from jax import lax
from jax.experimental import pallas as pl
from jax.experimental.pallas import tpu as pltpu

F32 = jnp.float32
BF16 = jnp.bfloat16

D_MODEL = 1024
CHUNK = 64
N_BRANCH = 3
POOL_WINDOWS = (2, 4, 8, 16)
POOL_GW = D_MODEL // len(POOL_WINDOWS)
POOL_HALO = 16
MLA_HEADS = 8
QK_NOPE = 128
QK_ROPE = 64
V_DIM = 128
Q_LORA = D_MODEL // 4
KV_LORA = D_MODEL // 8
ROPE_THETA = 10000.0
SG_BLOCK = 128
SG_GROUPS = 8
SG_GW = D_MODEL // SG_GROUPS
D_FF = 2816
CONV_W = 3
CONV_HALO = 8
LN_EPS = 1e-5
RMS_EPS = 1e-6
LANES = 128
ATT_SCALE = (QK_NOPE + QK_ROPE) ** -0.5 * 1.4426950408889634
MASK_VALUE = -1e30

C_GATE = 0
C_POOL = C_GATE + N_BRANCH * D_MODEL
C_LAT = C_POOL + D_MODEL
LAT_W = Q_LORA + KV_LORA + 4 * QK_ROPE
C_UV = C_LAT + LAT_W
IN_W = C_UV + 2 * D_MODEL

VMEM_LIMIT_BYTES = 60 * 1024 * 1024

ROW_GROUP = 256
TS_IN = 512
GATE_CHUNK = 512
UV_CHUNK = 512
TQ_ATT = ROW_GROUP
TM_OUT = 1024
TS_FFN = 512
FFN_CHUNK = 256


def _layer_norm(x):
    mu = jnp.mean(x, axis=-1, keepdims=True)
    xc = x - mu
    var = jnp.mean(xc * xc, axis=-1, keepdims=True)
    return xc * lax.rsqrt(var + LN_EPS)


def _rms_norm(x, g):
    return x * lax.rsqrt(jnp.mean(x * x, axis=-1, keepdims=True) + RMS_EPS) * g


def _dot(a, b):
    return jnp.dot(a, b, preferred_element_type=F32)


def _const_spec(shape):
    nd = len(shape)
    return pl.BlockSpec(shape, lambda *_: (0,) * nd, pipeline_mode=pl.Buffered(1))


def _mod_kernel(c_ref, w_ref, b_ref, o_ref):
    c = c_ref[...]
    c_act = c * jax.nn.sigmoid(c)
    o_ref[...] = jnp.dot(c_act, w_ref[...], preferred_element_type=F32,
                         precision=lax.Precision.HIGHEST) + b_ref[...]


def _modulation(c, w_ada, b_ada):
    depth, d, six_d = w_ada.shape
    nb = c.shape[0]
    out = pl.pallas_call(
        _mod_kernel,
        grid=(depth, six_d // d),
        in_specs=[
            pl.BlockSpec((nb, d), lambda l, j: (0, 0)),
            pl.BlockSpec((None, d, d), lambda l, j: (l, 0, j)),
            pl.BlockSpec((None, 1, d), lambda l, j: (l, 0, j)),
        ],
        out_specs=pl.BlockSpec((None, nb, d), lambda l, j: (l, 0, j)),
        out_shape=jax.ShapeDtypeStruct((depth, nb, six_d), F32),
        name="adaln_mod",
    )(c, w_ada, b_ada.reshape(depth, 1, six_d))
    return out.reshape(depth, nb, 6, d)


def _mixer_in_kernel(x_ref, mod_ref, cos_ref, sin_ref, w_in_ref, w_pool_ref, s_pool_ref,
                     g_q_ref, w_qn_ref, w_qr_ref, w_qrs_ref, g_kv_ref,
                     g_sg_ref, b_sg_ref, w_s_ref, b_s_ref,
                     gates_ref, ya_ref, q_ref, k_ref, kt_ref, yc_ref, halo):
    ts = x_ref.shape[0]
    hr = ROW_GROUP
    n_groups = ts // hr
    s_idx = pl.program_id(1)
    d = D_MODEL
    n_gate = N_BRANCH * d // GATE_CHUNK
    n_blk = hr // SG_BLOCK

    @pl.when(s_idx == 0)
    def _():
        halo[...] = jnp.zeros((POOL_HALO, d), F32)

    def prologue(r0):
        h = _layer_norm(x_ref[r0:r0 + hr, :]) * (1.0 + mod_ref[1:2, :]) + mod_ref[0:1, :]
        return h.astype(BF16)

    def gelu(t):
        return 0.5 * t * (1.0 + lax.erf(t * (2.0 ** -0.5)))

    def first_proj(hb):
        return (_dot(hb, w_in_ref[:, C_POOL:C_POOL + d]), _dot(hb, w_in_ref[:, C_LAT:C_LAT + LAT_W]))

    def body(r0, hb, a, lat):
        rows = slice(r0, r0 + hr)

        def proj(start, width):
            return _dot(hb, w_in_ref[:, start:start + width])

        pos = lax.broadcasted_iota(jnp.int32, (hr, 1), 0) + (s_idx * ts + r0)
        pooled = []
        for g, w in enumerate(POOL_WINDOWS):
            cols = slice(g * POOL_GW, (g + 1) * POOL_GW)
            cur = a[:, cols]
            win = jnp.concatenate([halo[:, cols], cur], axis=0)
            span = 1
            while span < w:
                win = win + pltpu.roll(win, span, 0)
                span *= 2
            cnt = jnp.minimum(pos + 1, w).astype(F32)
            pooled.append((win[POOL_HALO:POOL_HALO + hr] / cnt - cur).astype(BF16))
        halo[...] = a[hr - POOL_HALO:hr, :]

        uv_v = [proj(C_UV + d + c * UV_CHUNK, UV_CHUNK) for c in range(d // UV_CHUNK)]

        cos = cos_ref[rows, :]
        sin = sin_ref[rows, :]
        cqn = _rms_norm(lat[:, 0:Q_LORA], g_q_ref[...]).astype(BF16)
        ckvn = _rms_norm(lat[:, Q_LORA:Q_LORA + KV_LORA], g_kv_ref[...])
        o_kr = Q_LORA + KV_LORA
        kr = (lat[:, o_kr:o_kr + 2 * QK_ROPE] * cos
              + lat[:, o_kr + 2 * QK_ROPE:o_kr + 4 * QK_ROPE] * sin)
        k_tile = jnp.concatenate([ckvn.astype(BF16), kr.astype(BF16)], axis=1)
        k_ref[rows, :] = k_tile
        for n in range(hr // TQ_ATT):
            kt_ref[r0 // TQ_ATT + n] = k_tile[n * TQ_ATT:(n + 1) * TQ_ATT, :].T

        mixed = [_dot(pooled[g], w_pool_ref[g]) for g in range(len(POOL_WINDOWS))]
        q_lat = _dot(cqn, w_qn_ref[...])
        q_r = _dot(cqn, w_qr_ref[...])
        q_rs = _dot(cqn, w_qrs_ref[...])

        v = jnp.concatenate([gelu(t) for t in uv_v], axis=1)
        v = (_layer_norm(v) * g_sg_ref[...] + b_sg_ref[...]).astype(BF16)

        uv_u = [proj(C_UV + c * UV_CHUNK, UV_CHUNK) for c in range(d // UV_CHUNK)]

        for g in range(len(POOL_WINDOWS)):
            cols = slice(g * POOL_GW, (g + 1) * POOL_GW)
            ya_ref[rows, cols] = (mixed[g] * s_pool_ref[:, cols]).astype(BF16)
        q_lat = (q_lat * ATT_SCALE).astype(BF16)
        reps = MLA_HEADS * QK_ROPE // cos.shape[1]
        cos_t = jnp.concatenate([cos] * reps, axis=1)
        sin_t = jnp.concatenate([sin] * reps, axis=1)
        q_rope = ((q_r * cos_t + q_rs * sin_t) * ATT_SCALE).astype(BF16)
        lane = lax.broadcasted_iota(jnp.int32, (hr, 2 * QK_ROPE), 1)
        for hd in range(MLA_HEADS):
            pair = q_rope[:, (hd // 2) * 2 * QK_ROPE:(hd // 2 + 1) * 2 * QK_ROPE]
            keep = (lane < QK_ROPE) if hd % 2 == 0 else (lane >= QK_ROPE)
            q_ref[r0 // hr, hd * hr:(hd + 1) * hr, 0:KV_LORA] = q_lat[:, hd * KV_LORA:(hd + 1) * KV_LORA]
            q_ref[r0 // hr, hd * hr:(hd + 1) * hr, KV_LORA:] = jnp.where(keep, pair, jnp.zeros_like(pair))

        gate_z = [proj(C_GATE + c * GATE_CHUNK, GATE_CHUNK) for c in range(2)]

        u = jnp.concatenate([gelu(t) for t in uv_u], axis=1)

        tri_r = lax.broadcasted_iota(jnp.int32, (SG_BLOCK, SG_BLOCK), 0)
        tri_c = lax.broadcasted_iota(jnp.int32, (SG_BLOCK, SG_BLOCK), 1)
        z = []
        for g in range(SG_GROUPS):
            cols = slice(g * SG_GW, (g + 1) * SG_GW)
            w_tril = jnp.where(tri_c <= tri_r, w_s_ref[g], 0.0).astype(BF16)
            v_cat = jnp.concatenate(
                [v[n * SG_BLOCK:(n + 1) * SG_BLOCK, cols] for n in range(n_blk)], axis=1)
            z.append(_dot(w_tril, v_cat))

        def store_gates(c, t):
            gates_ref[rows, c * GATE_CHUNK:(c + 1) * GATE_CHUNK] = jax.nn.sigmoid(t).astype(BF16)

        for c in range(2, n_gate, 2):
            nxt = [proj(C_GATE + cc * GATE_CHUNK, GATE_CHUNK) for cc in range(c, min(c + 2, n_gate))]
            for k, t in enumerate(gate_z):
                store_gates(c - 2 + k, t)
            if c == 2:
                for g in range(SG_GROUPS):
                    cols = slice(g * SG_GW, (g + 1) * SG_GW)
                    for n in range(n_blk):
                        zn = z[g][:, n * SG_GW:(n + 1) * SG_GW] + b_s_ref[:, cols]
                        blk = slice(n * SG_BLOCK, (n + 1) * SG_BLOCK)
                        yc_ref[r0 + n * SG_BLOCK:r0 + (n + 1) * SG_BLOCK, cols] = (
                            u[blk, cols] * zn).astype(BF16)
            gate_z = nxt

        def tail():
            for k, t in enumerate(gate_z):
                store_gates(n_gate - len(gate_z) + k, t)
        return tail

    hb = prologue(0)
    first = first_proj(hb)
    tail = None
    for k in range(n_groups):
        hb_cur, first_cur = hb, first
        if k + 1 < n_groups:
            hb = prologue((k + 1) * hr)
        if tail is not None:
            tail()
        tail = body(k * hr, hb_cur, *first_cur)
        if k + 1 < n_groups:
            first = first_proj(hb)
    tail()


def _mixer_in(x, mod_l, cos, sin, w):
    nb, seq, d = x.shape
    ts = TS_IN
    tile = lambda width: pl.BlockSpec((None, ts, width), lambda b, s: (b, s, 0))
    consts = [w["w_in"], w["w_pool"], w["s_pool"], w["g_q"], w["w_qn"], w["w_qr"], w["w_qrs"],
              w["g_kv"], w["g_sg"], w["b_sg"], w["w_s"], w["b_s"]]
    k_w = KV_LORA + 2 * QK_ROPE
    return pl.pallas_call(
        _mixer_in_kernel,
        grid=(nb, seq // ts),
        in_specs=[tile(d),
                  pl.BlockSpec((None, 6, d), lambda b, s: (b, 0, 0)),
                  tile(cos.shape[-1]), tile(sin.shape[-1])]
                 + [_const_spec(a.shape) for a in consts],
        out_specs=[tile(N_BRANCH * d), tile(d),
                   pl.BlockSpec((None, ts // ROW_GROUP, MLA_HEADS * ROW_GROUP, k_w),
                                lambda b, s: (b, s, 0, 0)),
                   tile(k_w),
                   pl.BlockSpec((None, ts // TQ_ATT, k_w, TQ_ATT), lambda b, s: (b, s, 0, 0)),
                   tile(d)],
        out_shape=[jax.ShapeDtypeStruct((nb, seq, N_BRANCH * d), BF16),
                   jax.ShapeDtypeStruct((nb, seq, d), BF16),
                   jax.ShapeDtypeStruct((nb, seq // ROW_GROUP, MLA_HEADS * ROW_GROUP, k_w), BF16),
                   jax.ShapeDtypeStruct((nb, seq, k_w), BF16),
                   jax.ShapeDtypeStruct((nb, seq // TQ_ATT, k_w, TQ_ATT), BF16),
                   jax.ShapeDtypeStruct((nb, seq, d), BF16)],
        scratch_shapes=[pltpu.VMEM((POOL_HALO, d), F32)],
        compiler_params=pltpu.CompilerParams(
            dimension_semantics=("arbitrary", "arbitrary"),
            vmem_limit_bytes=VMEM_LIMIT_BYTES),
        name="mixer_in",
    )(x, mod_l, cos, sin, *consts)


def _attention_kernel(q_ref, kt_ref, k_ref, o_ref, m_scr, l_scr, acc_scr, s_scr):
    rows = q_ref.shape[0]
    tq = rows // MLA_HEADS
    tk = kt_ref.shape[2]
    i = pl.program_id(1)

    def scores(j):
        return _dot(q_ref[...], kt_ref[j])

    shift = CHUNK.bit_length() - 1
    q_chunk = lax.broadcasted_iota(jnp.int32, (tq, tk), 0) >> shift
    k_chunk = lax.broadcasted_iota(jnp.int32, (tq, tk), 1) >> shift
    bias = jnp.where(k_chunk <= q_chunk, 0.0, MASK_VALUE).astype(F32)

    def diag_scores():
        s = scores(i).reshape(MLA_HEADS, tq, tk) + bias[None]
        return s.reshape(rows, tk)

    def lane_tiles(s):
        return [s[:, c * LANES:(c + 1) * LANES] for c in range(s.shape[1] // LANES)]

    def row_max(s):
        part = functools.reduce(jnp.maximum, lane_tiles(s))
        return jnp.broadcast_to(jnp.max(part, axis=1, keepdims=True), (rows, LANES))

    def value_block(j):
        return k_ref[pl.ds(pl.multiple_of(j * tk, tk), tk), 0:KV_LORA]

    s0 = diag_scores()
    s_scr[...] = scores(0)
    m0 = row_max(s0)
    p_tiles = [jnp.exp2(t - m0) for t in lane_tiles(s0)]
    m_scr[...] = m0
    l_scr[...] = functools.reduce(jnp.add, p_tiles)
    acc_scr[...] = _dot(jnp.concatenate(p_tiles, axis=1).astype(BF16), value_block(i))

    def update(j, s):
        m_old = m_scr[...]
        m_new = jnp.maximum(m_old, row_max(s))
        alpha = jnp.exp2(m_old - m_new)
        p_tiles = [jnp.exp2(t - m_new) for t in lane_tiles(s)]
        l_scr[...] = alpha * l_scr[...] + functools.reduce(jnp.add, p_tiles)
        p = jnp.concatenate(p_tiles, axis=1).astype(BF16)
        acc_scr[...] = alpha * acc_scr[...] + _dot(p, value_block(j))
        m_scr[...] = m_new

    @pl.when(i > 0)
    def _():
        def body(j, carry):
            s_next = scores(j + 1)
            update(j, s_scr[...])
            s_scr[...] = s_next
            return carry

        lax.fori_loop(0, i - 1, body, 0)
        update(i - 1, s_scr[...])

    o_lat = (acc_scr[...] / jnp.sum(l_scr[...], axis=1, keepdims=True)).astype(BF16)
    for hd in range(MLA_HEADS):
        o_ref[:, hd * KV_LORA:(hd + 1) * KV_LORA] = o_lat[hd * tq:(hd + 1) * tq, :]


def _attention(q, k, kt):
    nb, n_tiles, rows, k_w = q.shape
    seq = k.shape[1]
    tq = rows // MLA_HEADS
    return pl.pallas_call(
        _attention_kernel,
        grid=(nb, n_tiles),
        in_specs=[pl.BlockSpec((None, None, rows, k_w), lambda b, i: (b, i, 0, 0)),
                  pl.BlockSpec((None, seq // tq, k_w, tq), lambda b, i: (b, 0, 0, 0)),
                  pl.BlockSpec((None, seq, k_w), lambda b, i: (b, 0, 0))],
        out_specs=pl.BlockSpec((None, tq, MLA_HEADS * KV_LORA), lambda b, i: (b, i, 0)),
        out_shape=jax.ShapeDtypeStruct((nb, seq, MLA_HEADS * KV_LORA), BF16),
        scratch_shapes=[pltpu.VMEM((rows, LANES), F32),
                        pltpu.VMEM((rows, LANES), F32),
                        pltpu.VMEM((rows, KV_LORA), F32),
                        pltpu.VMEM((rows, tq), F32)],
        compiler_params=pltpu.CompilerParams(
            dimension_semantics=("arbitrary", "arbitrary"),
            vmem_limit_bytes=VMEM_LIMIT_BYTES),
        name="latent_attention",
    )(q, kt, k)


def _mixer_out_kernel(x_ref, mod_ref, gates_ref, ya_ref, yb_ref, yc_ref, w_br_ref, w_o_ref,
                      ln_g_ref, ln_b_ref, o_ref, *, alpha):
    tm, d = x_ref.shape
    hr = ROW_GROUP
    n_groups = tm // hr
    branches = (ya_ref, yb_ref, yc_ref)

    def branch_dots(r0):
        return [_dot(y_ref[r0:r0 + hr, :], w_br_ref[br]) for br, y_ref in enumerate(branches)]

    def merge(r0, ys):
        merged = None
        for br, y in enumerate(ys):
            term = gates_ref[r0:r0 + hr, br * d:(br + 1) * d].astype(F32) * y
            merged = term if merged is None else merged + term
        return merged.astype(BF16)

    def epilogue(r0, y):
        r = alpha * x_ref[r0:r0 + hr, :] + mod_ref[2:3, :] * y
        o_ref[r0:r0 + hr, :] = _layer_norm(r) * ln_g_ref[...] + ln_b_ref[...]

    ys = branch_dots(0)
    pending = None
    for k in range(n_groups):
        ys_cur = ys
        if k + 1 < n_groups:
            ys = branch_dots((k + 1) * hr)
        merged = merge(k * hr, ys_cur)
        y = _dot(merged, w_o_ref[...])
        if pending is not None:
            epilogue(*pending)
        pending = (k * hr, y)
    epilogue(*pending)


def _mixer_out(x, mod_l, gates, ya, yb, yc, w_br, w_o, ln_g, ln_b, alpha):
    nb, seq, d = x.shape
    tm = TM_OUT
    tile = lambda width: pl.BlockSpec((None, tm, width), lambda b, s: (b, s, 0))
    return pl.pallas_call(
        functools.partial(_mixer_out_kernel, alpha=alpha),
        grid=(nb, seq // tm),
        in_specs=[tile(d),
                  pl.BlockSpec((None, 6, d), lambda b, s: (b, 0, 0)),
                  tile(N_BRANCH * d), tile(d), tile(d), tile(d),
                  _const_spec(w_br.shape), _const_spec(w_o.shape),
                  _const_spec(ln_g.shape), _const_spec(ln_b.shape)],
        out_specs=tile(d),
        out_shape=jax.ShapeDtypeStruct((nb, seq, d), F32),
        compiler_params=pltpu.CompilerParams(
            dimension_semantics=("arbitrary", "arbitrary"),
            vmem_limit_bytes=VMEM_LIMIT_BYTES),
        name="mixer_out",
    )(x, mod_l, gates, ya, yb, yc, w_br, w_o, ln_g, ln_b)


def _conv_ffn_kernel(x_ref, mod_ref, w_up_ref, cw_ref, cb_ref, w_down_ref, ln_g_ref, ln_b_ref,
                     o_ref, zhalo, *, alpha):
    ts = x_ref.shape[0]
    hr = ROW_GROUP
    n_groups = ts // hr
    s_idx = pl.program_id(1)
    bounds = list(range(0, D_FF, FFN_CHUNK)) + [D_FF]
    n_chunks = len(bounds) - 1

    @pl.when(s_idx == 0)
    def _():
        zhalo[...] = jnp.zeros((CONV_HALO, 2 * D_FF), F32)

    def prologue(r0):
        h = _layer_norm(x_ref[r0:r0 + hr, :]) * (1.0 + mod_ref[4:5, :]) + mod_ref[3:4, :]
        return h.astype(BF16)

    def chunk_cols(c):
        return (slice(bounds[c], bounds[c + 1]), slice(D_FF + bounds[c], D_FF + bounds[c + 1]))

    def up(hb, c):
        return [_dot(hb, w_up_ref[:, cols]) for cols in chunk_cols(c)]

    def conv(z, cols):
        ext = jnp.concatenate([zhalo[:, cols], z], axis=0)
        zc = cb_ref[:, cols] + cw_ref[CONV_W - 1:CONV_W, cols] * z
        for k in range(1, CONV_W):
            shifted = pltpu.roll(ext, k, 0)[CONV_HALO:CONV_HALO + hr]
            zc = zc + cw_ref[CONV_W - 1 - k:CONV_W - k, cols] * shifted
        zhalo[:, cols] = z[hr - CONV_HALO:hr]
        return zc

    def down(c, act):
        return _dot(act, w_down_ref[bounds[c]:bounds[c + 1], :])

    def epilogue(r0, y):
        r = alpha * x_ref[r0:r0 + hr, :] + mod_ref[5:6, :] * y
        o_ref[r0:r0 + hr, :] = _layer_norm(r) * ln_g_ref[...] + ln_b_ref[...]

    hb = prologue(0)
    z_next = up(hb, 0)
    pending = None
    for k in range(n_groups):
        hb_cur = hb
        if k + 1 < n_groups:
            hb = prologue((k + 1) * hr)
        y = jnp.zeros((hr, D_MODEL), F32)
        act_prev = None
        for c in range(n_chunks):
            z_cur = z_next
            if c + 1 < n_chunks:
                z_next = up(hb_cur, c + 1)
            elif k + 1 < n_groups:
                z_next = up(hb, 0)
            if act_prev is not None:
                y = y + down(c - 1, act_prev)
            if c == 0 and pending is not None:
                epilogue(*pending)
            val, gate = [conv(z, cols) for z, cols in zip(z_cur, chunk_cols(c))]
            act_prev = (gate * jax.nn.sigmoid(gate) * val).astype(BF16)
        y = y + down(n_chunks - 1, act_prev)
        pending = (k * hr, y)
    epilogue(*pending)


def _conv_ffn(x, mod_l, w_up, conv_w, conv_b, w_down, ln_g, ln_b, alpha):
    nb, seq, d = x.shape
    ts = TS_FFN
    tile = pl.BlockSpec((None, ts, d), lambda b, s: (b, s, 0))
    consts = [w_up, conv_w, conv_b, w_down, ln_g, ln_b]
    return pl.pallas_call(
        functools.partial(_conv_ffn_kernel, alpha=alpha),
        grid=(nb, seq // ts),
        in_specs=[tile, pl.BlockSpec((None, 6, d), lambda b, s: (b, 0, 0))]
                 + [_const_spec(a.shape) for a in consts],
        out_specs=tile,
        out_shape=jax.ShapeDtypeStruct((nb, seq, d), F32),
        scratch_shapes=[pltpu.VMEM((CONV_HALO, 2 * D_FF), F32)],
        compiler_params=pltpu.CompilerParams(
            dimension_semantics=("arbitrary", "arbitrary"),
            vmem_limit_bytes=VMEM_LIMIT_BYTES),
        name="conv_ffn",
    )(x, mod_l, w_up, conv_w, conv_b, w_down, ln_g, ln_b)


def _swap_halves(w):
    half = w.shape[-1] // 2
    return jnp.concatenate([w[..., half:], w[..., :half]], axis=-1)


def _rope_tables(pos):
    half = QK_ROPE // 2
    freqs = ROPE_THETA ** (-jnp.arange(0, QK_ROPE, 2, dtype=F32) / QK_ROPE)
    ang = pos.astype(F32)[..., None] * jnp.tile(freqs, LANES // half)
    sign = jnp.tile(jnp.concatenate([-jnp.ones((half,), F32), jnp.ones((half,), F32)]),
                    LANES // QK_ROPE)
    return jnp.cos(ang), jnp.sin(ang) * sign


def _fold_kernel(a_ref, b_ref, o_ref):
    o_ref[...] = jnp.dot(a_ref[...], b_ref[...], preferred_element_type=F32,
                         precision=lax.Precision.HIGHEST).astype(o_ref.dtype)


def _fold_heads(a, b):
    n, m, kk = a.shape
    nn = b.shape[2]
    return pl.pallas_call(
        _fold_kernel,
        grid=(n,),
        in_specs=[pl.BlockSpec((None, m, kk), lambda h: (h, 0, 0)),
                  pl.BlockSpec((None, kk, nn), lambda h: (h, 0, 0))],
        out_specs=pl.BlockSpec((None, m, nn), lambda h: (h, 0, 0)),
        out_shape=jax.ShapeDtypeStruct((n, m, nn), BF16),
        name="fold_weights",
    )(a, b)


def _absorbed_weights(w_uq, w_ukv, w_branch):
    depth = w_uq.shape[0]
    h, d = MLA_HEADS, D_MODEL
    uq_nope = jnp.transpose(w_uq[..., :QK_NOPE], (0, 2, 1, 3)).reshape(depth * h, Q_LORA, QK_NOPE)
    uk_t = jnp.transpose(w_ukv[..., :QK_NOPE], (0, 2, 3, 1)).reshape(depth * h, QK_NOPE, KV_LORA)
    w_q = _fold_heads(uq_nope, uk_t).reshape(depth, h, Q_LORA, KV_LORA)
    w_q = jnp.transpose(w_q, (0, 2, 1, 3)).reshape(depth, Q_LORA, h * KV_LORA)
    uv = jnp.transpose(w_ukv[..., QK_NOPE:], (0, 2, 1, 3)).reshape(depth * h, KV_LORA, V_DIM)
    w_b = _fold_heads(uv, w_branch[:, 1].reshape(depth * h, V_DIM, d)).reshape(depth, h * KV_LORA, d)
    return w_q, w_b


def _layer_weights(l, w_in, w_pool, s_pool, g_q, w_uq, g_kv, g_sg, b_sg, w_s, b_s, w_q_abs):
    d = D_MODEL
    off_cq = (N_BRANCH + 1) * d
    off_ckv = off_cq + Q_LORA
    off_kr = off_ckv + KV_LORA
    off_sg = off_kr + QK_ROPE
    wi = w_in[l]
    w_kr = wi[:, off_kr:off_sg]
    w_kr_sw = _swap_halves(w_kr)
    w_in_r = jnp.concatenate(
        [wi[:, :off_kr], w_kr, w_kr, w_kr_sw, w_kr_sw, wi[:, off_sg:]], axis=1).astype(BF16)
    w_q_rope = w_uq[l][:, :, QK_NOPE:]
    return dict(
        w_in=w_in_r,
        w_pool=w_pool[l].astype(BF16),
        s_pool=s_pool[l].reshape(1, d),
        g_q=g_q[l].reshape(1, Q_LORA),
        w_qn=w_q_abs[l],
        w_qr=w_q_rope.reshape(Q_LORA, MLA_HEADS * QK_ROPE).astype(BF16),
        w_qrs=_swap_halves(w_q_rope).reshape(Q_LORA, MLA_HEADS * QK_ROPE).astype(BF16),
        g_kv=g_kv[l].reshape(1, KV_LORA),
        g_sg=g_sg[l].reshape(1, d),
        b_sg=b_sg[l].reshape(1, d),
        w_s=w_s[l],
        b_s=jnp.repeat(b_s[l], SG_GW, axis=1),
    )


def kernel(x, c, pos, w_ada, b_ada, w_in, w_pool, s_pool, g_q, w_uq, g_kv, w_ukv, g_sg, b_sg, w_s, b_s, w_branch, w_o, ln_t_g, ln_t_b, w_up, conv_w, conv_b, w_down, ln_f_g, ln_f_b):
    depth = w_ada.shape[0]
    d = D_MODEL
    alpha = (2 * depth) ** 0.25
    assert x.shape[-1] == d and x.shape[1] % TM_OUT == 0
    cos, sin = _rope_tables(pos)
    mod = _modulation(c, w_ada, b_ada)
    w_q_abs, w_b_abs = _absorbed_weights(w_uq, w_ukv, w_branch)
    for l in range(depth):
        w = _layer_weights(l, w_in, w_pool, s_pool, g_q, w_uq, g_kv, g_sg, b_sg, w_s, b_s, w_q_abs)
        gates, ya, q, k, kt, yc = _mixer_in(x, mod[l], cos, sin, w)
        yb = _attention(q, k, kt)
        w_br = jnp.stack([w_branch[l, 0].astype(BF16), w_b_abs[l], w_branch[l, 2].astype(BF16)])
        x = _mixer_out(x, mod[l], gates, ya, yb, yc, w_br,
                       w_o[l].astype(BF16), ln_t_g[l].reshape(1, d), ln_t_b[l].reshape(1, d), alpha)
        x = _conv_ffn(x, mod[l], w_up[l].astype(BF16), conv_w[l], conv_b[l].reshape(1, 2 * D_FF),
                      w_down[l].astype(BF16), ln_f_g[l].reshape(1, d), ln_f_b[l].reshape(1, d), alpha)
    return x
```

```python
import functools

import jax
import jax.numpy as jnp
from jax import lax
from jax.experimental import pallas as pl
from jax.experimental.pallas import tpu as pltpu

F32 = jnp.float32
BF16 = jnp.bfloat16

D_MODEL = 1024
CHUNK = 64
N_BRANCH = 3
POOL_WINDOWS = (2, 4, 8, 16)
POOL_GW = D_MODEL // len(POOL_WINDOWS)
POOL_HALO = 16
MLA_HEADS = 8
QK_NOPE = 128
QK_ROPE = 64
V_DIM = 128
Q_LORA = D_MODEL // 4
KV_LORA = D_MODEL // 8
ROPE_THETA = 10000.0
SG_BLOCK = 128
SG_GROUPS = 8
SG_GW = D_MODEL // SG_GROUPS
D_FF = 2816
CONV_W = 3
CONV_HALO = 8
LN_EPS = 1e-5
RMS_EPS = 1e-6
LANES = 128
ATT_SCALE = (QK_NOPE + QK_ROPE) ** -0.5 * 1.4426950408889634
MASK_VALUE = -1e30

C_GATE = 0
C_POOL = C_GATE + N_BRANCH * D_MODEL
C_LAT = C_POOL + D_MODEL
LAT_W = Q_LORA + KV_LORA + 4 * QK_ROPE
C_UV = C_LAT + LAT_W
IN_W = C_UV + 2 * D_MODEL

VMEM_LIMIT_BYTES = 60 * 1024 * 1024

ROW_GROUP = 256
TS_IN = 512
GATE_CHUNK = 512
UV_CHUNK = 512
TQ_ATT = ROW_GROUP
ATT_GROUPS = 4
TM_OUT = 1024
TS_FFN = 512
FFN_CHUNK = 256


def _layer_norm(x):
    mu = jnp.mean(x, axis=-1, keepdims=True)
    xc = x - mu
    var = jnp.mean(xc * xc, axis=-1, keepdims=True)
    return xc * lax.rsqrt(var + LN_EPS)


def _rms_norm(x, g):
    return x * lax.rsqrt(jnp.mean(x * x, axis=-1, keepdims=True) + RMS_EPS) * g


def _dot(a, b):
    return jnp.dot(a, b, preferred_element_type=F32)


def _const_spec(shape):
    nd = len(shape)
    return pl.BlockSpec(shape, lambda *_: (0,) * nd, pipeline_mode=pl.Buffered(1))


def _mod_kernel(c_ref, w_ref, b_ref, o_ref):
    c = c_ref[...]
    c_act = c * jax.nn.sigmoid(c)
    o_ref[...] = jnp.dot(c_act, w_ref[...], preferred_element_type=F32,
                         precision=lax.Precision.HIGHEST) + b_ref[...]


def _modulation(c, w_ada, b_ada):
    depth, d, six_d = w_ada.shape
    nb = c.shape[0]
    out = pl.pallas_call(
        _mod_kernel,
        grid=(depth, six_d // d),
        in_specs=[
            pl.BlockSpec((nb, d), lambda l, j: (0, 0)),
            pl.BlockSpec((None, d, d), lambda l, j: (l, 0, j)),
            pl.BlockSpec((None, 1, d), lambda l, j: (l, 0, j)),
        ],
        out_specs=pl.BlockSpec((None, nb, d), lambda l, j: (l, 0, j)),
        out_shape=jax.ShapeDtypeStruct((depth, nb, six_d), F32),
        name="adaln_mod",
    )(c, w_ada, b_ada.reshape(depth, 1, six_d))
    return out.reshape(depth, nb, 6, d)


def _mixer_in_kernel(x_ref, mod_ref, cos_ref, sin_ref, w_in_ref, w_pool_ref, s_pool_ref,
                     g_q_ref, w_qn_ref, w_qr_ref, w_qrs_ref, g_kv_ref,
                     g_sg_ref, b_sg_ref, w_s_ref, b_s_ref,
                     gates_ref, ya_ref, q_ref, k_ref, kt_ref, yc_ref, halo):
    ts = x_ref.shape[0]
    hr = ROW_GROUP
    n_groups = ts // hr
    s_idx = pl.program_id(1)
    d = D_MODEL
    n_gate = N_BRANCH * d // GATE_CHUNK
    n_blk = hr // SG_BLOCK

    @pl.when(s_idx == 0)
    def _():
        halo[...] = jnp.zeros((POOL_HALO, d), F32)

    def prologue(r0):
        h = _layer_norm(x_ref[r0:r0 + hr, :]) * (1.0 + mod_ref[1:2, :]) + mod_ref[0:1, :]
        return h.astype(BF16)

    def gelu(t):
        return 0.5 * t * (1.0 + lax.erf(t * (2.0 ** -0.5)))

    def first_proj(hb):
        return (_dot(hb, w_in_ref[:, C_POOL:C_POOL + d]), _dot(hb, w_in_ref[:, C_LAT:C_LAT + LAT_W]))

    def body(r0, hb, a, lat):
        rows = slice(r0, r0 + hr)

        def proj(start, width):
            return _dot(hb, w_in_ref[:, start:start + width])

        pos = lax.broadcasted_iota(jnp.int32, (hr, 1), 0) + (s_idx * ts + r0)
        pooled = []
        for g, w in enumerate(POOL_WINDOWS):
            cols = slice(g * POOL_GW, (g + 1) * POOL_GW)
            cur = a[:, cols]
            win = jnp.concatenate([halo[:, cols], cur], axis=0)
            span = 1
            while span < w:
                win = win + pltpu.roll(win, span, 0)
                span *= 2
            cnt = jnp.minimum(pos + 1, w).astype(F32)
            pooled.append((win[POOL_HALO:POOL_HALO + hr] / cnt - cur).astype(BF16))
        halo[...] = a[hr - POOL_HALO:hr, :]

        uv_v = [proj(C_UV + d + c * UV_CHUNK, UV_CHUNK) for c in range(d // UV_CHUNK)]

        cos = cos_ref[rows, :]
        sin = sin_ref[rows, :]
        cqn = _rms_norm(lat[:, 0:Q_LORA], g_q_ref[...]).astype(BF16)
        ckvn = _rms_norm(lat[:, Q_LORA:Q_LORA + KV_LORA], g_kv_ref[...])
        o_kr = Q_LORA + KV_LORA
        kr = (lat[:, o_kr:o_kr + 2 * QK_ROPE] * cos
              + lat[:, o_kr + 2 * QK_ROPE:o_kr + 4 * QK_ROPE] * sin)
        k_tile = jnp.concatenate([ckvn.astype(BF16), kr.astype(BF16)], axis=1)
        k_ref[rows, :] = k_tile
        for n in range(hr // TQ_ATT):
            kt_ref[r0 // TQ_ATT + n] = k_tile[n * TQ_ATT:(n + 1) * TQ_ATT, :].T

        mixed = [_dot(pooled[g], w_pool_ref[g]) for g in range(len(POOL_WINDOWS))]
        q_lat = _dot(cqn, w_qn_ref[...])
        q_r = _dot(cqn, w_qr_ref[...])
        q_rs = _dot(cqn, w_qrs_ref[...])

        v = jnp.concatenate([gelu(t) for t in uv_v], axis=1)
        v = (_layer_norm(v) * g_sg_ref[...] + b_sg_ref[...]).astype(BF16)

        uv_u = [proj(C_UV + c * UV_CHUNK, UV_CHUNK) for c in range(d // UV_CHUNK)]

        for g in range(len(POOL_WINDOWS)):
            cols = slice(g * POOL_GW, (g + 1) * POOL_GW)
            ya_ref[rows, cols] = (mixed[g] * s_pool_ref[:, cols]).astype(BF16)
        q_lat = (q_lat * ATT_SCALE).astype(BF16)
        reps = MLA_HEADS * QK_ROPE // cos.shape[1]
        cos_t = jnp.concatenate([cos] * reps, axis=1)
        sin_t = jnp.concatenate([sin] * reps, axis=1)
        q_rope = ((q_r * cos_t + q_rs * sin_t) * ATT_SCALE).astype(BF16)
        lane = lax.broadcasted_iota(jnp.int32, (hr, 2 * QK_ROPE), 1)
        for hd in range(MLA_HEADS):
            pair = q_rope[:, (hd // 2) * 2 * QK_ROPE:(hd // 2 + 1) * 2 * QK_ROPE]
            keep = (lane < QK_ROPE) if hd % 2 == 0 else (lane >= QK_ROPE)
            q_ref[r0 // hr, hd * hr:(hd + 1) * hr, 0:KV_LORA] = q_lat[:, hd * KV_LORA:(hd + 1) * KV_LORA]
            q_ref[r0 // hr, hd * hr:(hd + 1) * hr, KV_LORA:] = jnp.where(keep, pair, jnp.zeros_like(pair))

        gate_z = [proj(C_GATE + c * GATE_CHUNK, GATE_CHUNK) for c in range(2)]

        u = jnp.concatenate([gelu(t) for t in uv_u], axis=1)

        tri_r = lax.broadcasted_iota(jnp.int32, (SG_BLOCK, SG_BLOCK), 0)
        tri_c = lax.broadcasted_iota(jnp.int32, (SG_BLOCK, SG_BLOCK), 1)
        z = []
        for g in range(SG_GROUPS):
            cols = slice(g * SG_GW, (g + 1) * SG_GW)
            w_tril = jnp.where(tri_c <= tri_r, w_s_ref[g], 0.0).astype(BF16)
            v_cat = jnp.concatenate(
                [v[n * SG_BLOCK:(n + 1) * SG_BLOCK, cols] for n in range(n_blk)], axis=1)
            z.append(_dot(w_tril, v_cat))

        def store_gates(c, t):
            gates_ref[rows, c * GATE_CHUNK:(c + 1) * GATE_CHUNK] = jax.nn.sigmoid(t).astype(BF16)

        for c in range(2, n_gate, 2):
            nxt = [proj(C_GATE + cc * GATE_CHUNK, GATE_CHUNK) for cc in range(c, min(c + 2, n_gate))]
            for k, t in enumerate(gate_z):
                store_gates(c - 2 + k, t)
            if c == 2:
                for g in range(SG_GROUPS):
                    cols = slice(g * SG_GW, (g + 1) * SG_GW)
                    for n in range(n_blk):
                        zn = z[g][:, n * SG_GW:(n + 1) * SG_GW] + b_s_ref[:, cols]
                        blk = slice(n * SG_BLOCK, (n + 1) * SG_BLOCK)
                        yc_ref[r0 + n * SG_BLOCK:r0 + (n + 1) * SG_BLOCK, cols] = (
                            u[blk, cols] * zn).astype(BF16)
            gate_z = nxt

        def tail():
            for k, t in enumerate(gate_z):
                store_gates(n_gate - len(gate_z) + k, t)
        return tail

    hb = prologue(0)
    first = first_proj(hb)
    tail = None
    for k in range(n_groups):
        hb_cur, first_cur = hb, first
        if k + 1 < n_groups:
            hb = prologue((k + 1) * hr)
        if tail is not None:
            tail()
        tail = body(k * hr, hb_cur, *first_cur)
        if k + 1 < n_groups:
            first = first_proj(hb)
    tail()


def _mixer_in(x, mod_l, cos, sin, w):
    nb, seq, d = x.shape
    ts = TS_IN
    tile = lambda width: pl.BlockSpec((None, ts, width), lambda b, s: (b, s, 0))
    consts = [w["w_in"], w["w_pool"], w["s_pool"], w["g_q"], w["w_qn"], w["w_qr"], w["w_qrs"],
              w["g_kv"], w["g_sg"], w["b_sg"], w["w_s"], w["b_s"]]
    k_w = KV_LORA + 2 * QK_ROPE
    return pl.pallas_call(
        _mixer_in_kernel,
        grid=(nb, seq // ts),
        in_specs=[tile(d),
                  pl.BlockSpec((None, 6, d), lambda b, s: (b, 0, 0)),
                  tile(cos.shape[-1]), tile(sin.shape[-1])]
                 + [_const_spec(a.shape) for a in consts],
        out_specs=[tile(N_BRANCH * d), tile(d),
                   pl.BlockSpec((None, ts // ROW_GROUP, MLA_HEADS * ROW_GROUP, k_w),
                                lambda b, s: (b, s, 0, 0)),
                   tile(k_w),
                   pl.BlockSpec((None, ts // TQ_ATT, k_w, TQ_ATT), lambda b, s: (b, s, 0, 0)),
                   tile(d)],
        out_shape=[jax.ShapeDtypeStruct((nb, seq, N_BRANCH * d), BF16),
                   jax.ShapeDtypeStruct((nb, seq, d), BF16),
                   jax.ShapeDtypeStruct((nb, seq // ROW_GROUP, MLA_HEADS * ROW_GROUP, k_w), BF16),
                   jax.ShapeDtypeStruct((nb, seq, k_w), BF16),
                   jax.ShapeDtypeStruct((nb, seq // TQ_ATT, k_w, TQ_ATT), BF16),
                   jax.ShapeDtypeStruct((nb, seq, d), BF16)],
        scratch_shapes=[pltpu.VMEM((POOL_HALO, d), F32)],
        compiler_params=pltpu.CompilerParams(
            dimension_semantics=("arbitrary", "arbitrary"),
            vmem_limit_bytes=VMEM_LIMIT_BYTES),
        name="mixer_in",
    )(x, mod_l, cos, sin, *consts)


def _attention_kernel(q_ref, kt_ref, k_ref, o_ref, m_scr, l_scr, acc_scr, s_scr):
    n_grp, g_rows, k_w = q_ref.shape
    rows = n_grp * g_rows
    tq = g_rows // MLA_HEADS
    tk = kt_ref.shape[2]
    i = pl.program_id(1)

    def scores(j, r0=0):
        return _dot(q_ref[...].reshape(rows, k_w)[r0:, :], kt_ref[j])

    shift = CHUNK.bit_length() - 1
    q_chunk = lax.broadcasted_iota(jnp.int32, (tq, tk), 0) >> shift
    k_chunk = lax.broadcasted_iota(jnp.int32, (tq, tk), 1) >> shift
    bias = jnp.where(k_chunk <= q_chunk, 0.0, MASK_VALUE).astype(F32)

    def lane_tiles(s):
        return [s[:, c * LANES:(c + 1) * LANES] for c in range(s.shape[1] // LANES)]

    def row_max(s):
        part = functools.reduce(jnp.maximum, lane_tiles(s))
        return jnp.broadcast_to(jnp.max(part, axis=1, keepdims=True), (s.shape[0], LANES))

    def value_block(j):
        return k_ref[pl.ds(pl.multiple_of(j * tk, tk), tk), 0:KV_LORA]

    def diag_scores(g):
        s = scores(n_grp * i + g, g * g_rows)
        first = s[0:g_rows].reshape(MLA_HEADS, tq, tk) + bias[None]
        first = first.reshape(g_rows, tk)
        return first if g == n_grp - 1 else jnp.concatenate([first, s[g_rows:]], axis=0)

    def update(j, s, r0=0):
        m_old = m_scr[r0:, :]
        m_new = jnp.maximum(m_old, row_max(s))
        alpha = jnp.exp2(m_old - m_new)
        p_tiles = [jnp.exp2(t - m_new) for t in lane_tiles(s)]
        l_scr[r0:, :] = alpha * l_scr[r0:, :] + functools.reduce(jnp.add, p_tiles)
        p = jnp.concatenate(p_tiles, axis=1).astype(BF16)
        acc_scr[r0:, :] = alpha * acc_scr[r0:, :] + _dot(p, value_block(j))
        m_scr[r0:, :] = m_new

    s0 = diag_scores(0)
    s_scr[...] = scores(0)
    m0 = row_max(s0)
    p_tiles = [jnp.exp2(t - m0) for t in lane_tiles(s0)]
    m_scr[...] = m0
    l_scr[...] = functools.reduce(jnp.add, p_tiles)
    acc_scr[...] = _dot(jnp.concatenate(p_tiles, axis=1).astype(BF16), value_block(n_grp * i))
    for g in range(1, n_grp):
        update(n_grp * i + g, diag_scores(g), g * g_rows)

    n_full = n_grp * i

    @pl.when(n_full > 0)
    def _():
        def body(j, carry):
            s_next = scores(j + 1)
            update(j, s_scr[...])
            s_scr[...] = s_next
            return carry

        lax.fori_loop(0, n_full - 1, body, 0)
        update(n_full - 1, s_scr[...])

    o_lat = (acc_scr[...] / jnp.sum(l_scr[...], axis=1, keepdims=True)).astype(BF16)
    for g in range(n_grp):
        for hd in range(MLA_HEADS):
            r0 = g * g_rows + hd * tq
            o_ref[g * tq:(g + 1) * tq, hd * KV_LORA:(hd + 1) * KV_LORA] = o_lat[r0:r0 + tq, :]


def _attention(q, k, kt):
    nb, n_tiles, g_rows, k_w = q.shape
    seq = k.shape[1]
    tq = g_rows // MLA_HEADS
    n_grp = ATT_GROUPS
    rows = n_grp * g_rows
    return pl.pallas_call(
        _attention_kernel,
        grid=(nb, n_tiles // n_grp),
        in_specs=[pl.BlockSpec((None, n_grp, g_rows, k_w), lambda b, i: (b, i, 0, 0)),
                  pl.BlockSpec((None, seq // tq, k_w, tq), lambda b, i: (b, 0, 0, 0)),
                  pl.BlockSpec((None, seq, k_w), lambda b, i: (b, 0, 0))],
        out_specs=pl.BlockSpec((None, n_grp * tq, MLA_HEADS * KV_LORA), lambda b, i: (b, i, 0)),
        out_shape=jax.ShapeDtypeStruct((nb, seq, MLA_HEADS * KV_LORA), BF16),
        scratch_shapes=[pltpu.VMEM((rows, LANES), F32),
                        pltpu.VMEM((rows, LANES), F32),
                        pltpu.VMEM((rows, KV_LORA), F32),
                        pltpu.VMEM((rows, tq), F32)],
        compiler_params=pltpu.CompilerParams(
            dimension_semantics=("arbitrary", "arbitrary"),
            vmem_limit_bytes=VMEM_LIMIT_BYTES),
        name="latent_attention",
    )(q, kt, k)


def _mixer_out_kernel(x_ref, mod_ref, gates_ref, ya_ref, yb_ref, yc_ref, w_br_ref, w_o_ref,
                      ln_g_ref, ln_b_ref, o_ref, *, alpha):
    tm, d = x_ref.shape
    hr = ROW_GROUP
    n_groups = tm // hr
    branches = (ya_ref, yb_ref, yc_ref)

    def branch_dots(r0):
        return [_dot(y_ref[r0:r0 + hr, :], w_br_ref[br]) for br, y_ref in enumerate(branches)]

    def merge(r0, ys):
        merged = None
        for br, y in enumerate(ys):
            term = gates_ref[r0:r0 + hr, br * d:(br + 1) * d].astype(F32) * y
            merged = term if merged is None else merged + term
        return merged.astype(BF16)

    def epilogue(r0, y):
        r = alpha * x_ref[r0:r0 + hr, :] + mod_ref[2:3, :] * y
        o_ref[r0:r0 + hr, :] = _layer_norm(r) * ln_g_ref[...] + ln_b_ref[...]

    ys = branch_dots(0)
    pending = None
    for k in range(n_groups):
        ys_cur = ys
        if k + 1 < n_groups:
            ys = branch_dots((k + 1) * hr)
        merged = merge(k * hr, ys_cur)
        y = _dot(merged, w_o_ref[...])
        if pending is not None:
            epilogue(*pending)
        pending = (k * hr, y)
    epilogue(*pending)


def _mixer_out(x, mod_l, gates, ya, yb, yc, w_br, w_o, ln_g, ln_b, alpha):
    nb, seq, d = x.shape
    tm = TM_OUT
    tile = lambda width: pl.BlockSpec((None, tm, width), lambda b, s: (b, s, 0))
    return pl.pallas_call(
        functools.partial(_mixer_out_kernel, alpha=alpha),
        grid=(nb, seq // tm),
        in_specs=[tile(d),
                  pl.BlockSpec((None, 6, d), lambda b, s: (b, 0, 0)),
                  tile(N_BRANCH * d), tile(d), tile(d), tile(d),
                  _const_spec(w_br.shape), _const_spec(w_o.shape),
                  _const_spec(ln_g.shape), _const_spec(ln_b.shape)],
        out_specs=tile(d),
        out_shape=jax.ShapeDtypeStruct((nb, seq, d), F32),
        compiler_params=pltpu.CompilerParams(
            dimension_semantics=("arbitrary", "arbitrary"),
            vmem_limit_bytes=VMEM_LIMIT_BYTES),
        name="mixer_out",
    )(x, mod_l, gates, ya, yb, yc, w_br, w_o, ln_g, ln_b)


def _conv_ffn_kernel(x_ref, mod_ref, w_up_ref, cw_ref, cb_ref, w_down_ref, ln_g_ref, ln_b_ref,
                     o_ref, zhalo, *, alpha):
    ts = x_ref.shape[0]
    hr = ROW_GROUP
    n_groups = ts // hr
    s_idx = pl.program_id(1)
    bounds = list(range(0, D_FF, FFN_CHUNK)) + [D_FF]
    n_chunks = len(bounds) - 1

    @pl.when(s_idx == 0)
    def _():
        zhalo[...] = jnp.zeros((CONV_HALO, 2 * D_FF), F32)

    def prologue(r0):
        h = _layer_norm(x_ref[r0:r0 + hr, :]) * (1.0 + mod_ref[4:5, :]) + mod_ref[3:4, :]
        return h.astype(BF16)

    def chunk_cols(c):
        return (slice(bounds[c], bounds[c + 1]), slice(D_FF + bounds[c], D_FF + bounds[c + 1]))

    def up(hb, c):
        return [_dot(hb, w_up_ref[:, cols]) for cols in chunk_cols(c)]

    def conv(z, cols):
        ext = jnp.concatenate([zhalo[:, cols], z], axis=0)
        zc = cb_ref[:, cols] + cw_ref[CONV_W - 1:CONV_W, cols] * z
        for k in range(1, CONV_W):
            shifted = pltpu.roll(ext, k, 0)[CONV_HALO:CONV_HALO + hr]
            zc = zc + cw_ref[CONV_W - 1 - k:CONV_W - k, cols] * shifted
        zhalo[:, cols] = z[hr - CONV_HALO:hr]
        return zc

    def down(c, act):
        return _dot(act, w_down_ref[bounds[c]:bounds[c + 1], :])

    def epilogue(r0, y):
        r = alpha * x_ref[r0:r0 + hr, :] + mod_ref[5:6, :] * y
        o_ref[r0:r0 + hr, :] = _layer_norm(r) * ln_g_ref[...] + ln_b_ref[...]

    hb = prologue(0)
    z_next = up(hb, 0)
    pending = None
    for k in range(n_groups):
        hb_cur = hb
        if k + 1 < n_groups:
            hb = prologue((k + 1) * hr)
        y = jnp.zeros((hr, D_MODEL), F32)
        act_prev = None
        for c in range(n_chunks):
            z_cur = z_next
            if c + 1 < n_chunks:
                z_next = up(hb_cur, c + 1)
            elif k + 1 < n_groups:
                z_next = up(hb, 0)
            if act_prev is not None:
                y = y + down(c - 1, act_prev)
            if c == 0 and pending is not None:
                epilogue(*pending)
            val, gate = [conv(z, cols) for z, cols in zip(z_cur, chunk_cols(c))]
            act_prev = (gate * jax.nn.sigmoid(gate) * val).astype(BF16)
        y = y + down(n_chunks - 1, act_prev)
        pending = (k * hr, y)
    epilogue(*pending)


def _conv_ffn(x, mod_l, w_up, conv_w, conv_b, w_down, ln_g, ln_b, alpha):
    nb, seq, d = x.shape
    ts = TS_FFN
    tile = pl.BlockSpec((None, ts, d), lambda b, s: (b, s, 0))
    consts = [w_up, conv_w, conv_b, w_down, ln_g, ln_b]
    return pl.pallas_call(
        functools.partial(_conv_ffn_kernel, alpha=alpha),
        grid=(nb, seq // ts),
        in_specs=[tile, pl.BlockSpec((None, 6, d), lambda b, s: (b, 0, 0))]
                 + [_const_spec(a.shape) for a in consts],
        out_specs=tile,
        out_shape=jax.ShapeDtypeStruct((nb, seq, d), F32),
        scratch_shapes=[pltpu.VMEM((CONV_HALO, 2 * D_FF), F32)],
        compiler_params=pltpu.CompilerParams(
            dimension_semantics=("arbitrary", "arbitrary"),
            vmem_limit_bytes=VMEM_LIMIT_BYTES),
        name="conv_ffn",
    )(x, mod_l, w_up, conv_w, conv_b, w_down, ln_g, ln_b)


def _swap_halves(w):
    half = w.shape[-1] // 2
    return jnp.concatenate([w[..., half:], w[..., :half]], axis=-1)


def _rope_tables(pos):
    half = QK_ROPE // 2
    freqs = ROPE_THETA ** (-jnp.arange(0, QK_ROPE, 2, dtype=F32) / QK_ROPE)
    ang = pos.astype(F32)[..., None] * jnp.tile(freqs, LANES // half)
    sign = jnp.tile(jnp.concatenate([-jnp.ones((half,), F32), jnp.ones((half,), F32)]),
                    LANES // QK_ROPE)
    return jnp.cos(ang), jnp.sin(ang) * sign


def _fold_kernel(a_ref, b_ref, o_ref):
    o_ref[...] = jnp.dot(a_ref[...], b_ref[...], preferred_element_type=F32,
                         precision=lax.Precision.HIGHEST).astype(o_ref.dtype)


def _fold_heads(a, b):
    n, m, kk = a.shape
    nn = b.shape[2]
    return pl.pallas_call(
        _fold_kernel,
        grid=(n,),
        in_specs=[pl.BlockSpec((None, m, kk), lambda h: (h, 0, 0)),
                  pl.BlockSpec((None, kk, nn), lambda h: (h, 0, 0))],
        out_specs=pl.BlockSpec((None, m, nn), lambda h: (h, 0, 0)),
        out_shape=jax.ShapeDtypeStruct((n, m, nn), BF16),
        name="fold_weights",
    )(a, b)


def _absorbed_weights(w_uq, w_ukv, w_branch):
    depth = w_uq.shape[0]
    h, d = MLA_HEADS, D_MODEL
    uq_nope = jnp.transpose(w_uq[..., :QK_NOPE], (0, 2, 1, 3)).reshape(depth * h, Q_LORA, QK_NOPE)
    uk_t = jnp.transpose(w_ukv[..., :QK_NOPE], (0, 2, 3, 1)).reshape(depth * h, QK_NOPE, KV_LORA)
    w_q = _fold_heads(uq_nope, uk_t).reshape(depth, h, Q_LORA, KV_LORA)
    w_q = jnp.transpose(w_q, (0, 2, 1, 3)).reshape(depth, Q_LORA, h * KV_LORA)
    uv = jnp.transpose(w_ukv[..., QK_NOPE:], (0, 2, 1, 3)).reshape(depth * h, KV_LORA, V_DIM)
    w_b = _fold_heads(uv, w_branch[:, 1].reshape(depth * h, V_DIM, d)).reshape(depth, h * KV_LORA, d)
    return w_q, w_b


def _layer_weights(l, w_in, w_pool, s_pool, g_q, w_uq, g_kv, g_sg, b_sg, w_s, b_s, w_q_abs):
    d = D_MODEL
    off_cq = (N_BRANCH + 1) * d
    off_ckv = off_cq + Q_LORA
    off_kr = off_ckv + KV_LORA
    off_sg = off_kr + QK_ROPE
    wi = w_in[l]
    w_kr = wi[:, off_kr:off_sg]
    w_kr_sw = _swap_halves(w_kr)
    w_in_r = jnp.concatenate(
        [wi[:, :off_kr], w_kr, w_kr, w_kr_sw, w_kr_sw, wi[:, off_sg:]], axis=1).astype(BF16)
    w_q_rope = w_uq[l][:, :, QK_NOPE:]
    return dict(
        w_in=w_in_r,
        w_pool=w_pool[l].astype(BF16),
        s_pool=s_pool[l].reshape(1, d),
        g_q=g_q[l].reshape(1, Q_LORA),
        w_qn=w_q_abs[l],
        w_qr=w_q_rope.reshape(Q_LORA, MLA_HEADS * QK_ROPE).astype(BF16),
        w_qrs=_swap_halves(w_q_rope).reshape(Q_LORA, MLA_HEADS * QK_ROPE).astype(BF16),
        g_kv=g_kv[l].reshape(1, KV_LORA),
        g_sg=g_sg[l].reshape(1, d),
        b_sg=b_sg[l].reshape(1, d),
        w_s=w_s[l],
        b_s=jnp.repeat(b_s[l], SG_GW, axis=1),
    )


def kernel(x, c, pos, w_ada, b_ada, w_in, w_pool, s_pool, g_q, w_uq, g_kv, w_ukv, g_sg, b_sg, w_s, b_s, w_branch, w_o, ln_t_g, ln_t_b, w_up, conv_w, conv_b, w_down, ln_f_g, ln_f_b):
    depth = w_ada.shape[0]
    d = D_MODEL
    alpha = (2 * depth) ** 0.25
    assert x.shape[-1] == d and x.shape[1] % TM_OUT == 0
    cos, sin = _rope_tables(pos)
    mod = _modulation(c, w_ada, b_ada)
    w_q_abs, w_b_abs = _absorbed_weights(w_uq, w_ukv, w_branch)
    for l in range(depth):
        w = _layer_weights(l, w_in, w_pool, s_pool, g_q, w_uq, g_kv, g_sg, b_sg, w_s, b_s, w_q_abs)
        gates, ya, q, k, kt, yc = _mixer_in(x, mod[l], cos, sin, w)
        yb = _attention(q, k, kt)
        w_br = jnp.stack([w_branch[l, 0].astype(BF16), w_b_abs[l], w_branch[l, 2].astype(BF16)])
        x = _mixer_out(x, mod[l], gates, ya, yb, yc, w_br,
                       w_o[l].astype(BF16), ln_t_g[l].reshape(1, d), ln_t_b[l].reshape(1, d), alpha)
        x = _conv_ffn(x, mod[l], w_up[l].astype(BF16), conv_w[l], conv_b[l].reshape(1, 2 * D_FF),
                      w_down[l].astype(BF16), ln_f_g[l].reshape(1, d), ln_f_b[l].reshape(1, d), alpha)
    return x
```

```python
import functools

import jax
import jax.numpy as jnp
from jax import lax
from jax.experimental import pallas as pl
from jax.experimental.pallas import tpu as pltpu

F32 = jnp.float32
BF16 = jnp.bfloat16

D_MODEL = 1024
CHUNK = 64
N_BRANCH = 3
POOL_WINDOWS = (2, 4, 8, 16)
POOL_GW = D_MODEL // len(POOL_WINDOWS)
POOL_HALO = 16
MLA_HEADS = 8
QK_NOPE = 128
QK_ROPE = 64
V_DIM = 128
Q_LORA = D_MODEL // 4
KV_LORA = D_MODEL // 8
ROPE_THETA = 10000.0
SG_BLOCK = 128
SG_GROUPS = 8
SG_GW = D_MODEL // SG_GROUPS
D_FF = 2816
CONV_W = 3
CONV_HALO = 8
LN_EPS = 1e-5
RMS_EPS = 1e-6
LANES = 128
ATT_SCALE = (QK_NOPE + QK_ROPE) ** -0.5 * 1.4426950408889634
MASK_VALUE = -1e30

C_GATE = 0
C_POOL = C_GATE + N_BRANCH * D_MODEL
C_LAT = C_POOL + D_MODEL
LAT_W = Q_LORA + KV_LORA + 4 * QK_ROPE
C_UV = C_LAT + LAT_W
IN_W = C_UV + 2 * D_MODEL

VMEM_LIMIT_BYTES = 60 * 1024 * 1024

ROW_GROUP = 256
TS_IN = 1024
GATE_CHUNK = 512
UV_CHUNK = 512
TQ_ATT = ROW_GROUP
ATT_GROUPS = 4
TM_OUT = 1024
TS_FFN = 512
FFN_CHUNK = 256


def _layer_norm(x):
    mu = jnp.mean(x, axis=-1, keepdims=True)
    xc = x - mu
    var = jnp.mean(xc * xc, axis=-1, keepdims=True)
    return xc * lax.rsqrt(var + LN_EPS)


def _rms_norm(x, g):
    return x * lax.rsqrt(jnp.mean(x * x, axis=-1, keepdims=True) + RMS_EPS) * g


def _dot(a, b):
    return jnp.dot(a, b, preferred_element_type=F32)


def _const_spec(shape):
    nd = len(shape)
    return pl.BlockSpec(shape, lambda *_: (0,) * nd, pipeline_mode=pl.Buffered(1))


def _mod_kernel(c_ref, w_ref, b_ref, o_ref):
    c = c_ref[...]
    c_act = c * jax.nn.sigmoid(c)
    o_ref[...] = jnp.dot(c_act, w_ref[...], preferred_element_type=F32,
                         precision=lax.Precision.HIGHEST) + b_ref[...]


def _modulation(c, w_ada, b_ada):
    depth, d, six_d = w_ada.shape
    nb = c.shape[0]
    out = pl.pallas_call(
        _mod_kernel,
        grid=(depth, six_d // d),
        in_specs=[
            pl.BlockSpec((nb, d), lambda l, j: (0, 0)),
            pl.BlockSpec((None, d, d), lambda l, j: (l, 0, j)),
            pl.BlockSpec((None, 1, d), lambda l, j: (l, 0, j)),
        ],
        out_specs=pl.BlockSpec((None, nb, d), lambda l, j: (l, 0, j)),
        out_shape=jax.ShapeDtypeStruct((depth, nb, six_d), F32),
        name="adaln_mod",
    )(c, w_ada, b_ada.reshape(depth, 1, six_d))
    return out.reshape(depth, nb, 6, d)


def _mixer_in_kernel(x_ref, mod_ref, cos_ref, sin_ref, w_in_ref, w_pool_ref, s_pool_ref,
                     g_q_ref, w_qn_ref, w_qr_ref, w_qrs_ref, g_kv_ref,
                     g_sg_ref, b_sg_ref, w_s_ref, b_s_ref,
                     gates_ref, ya_ref, q_ref, k_ref, kt_ref, yc_ref, halo):
    ts = x_ref.shape[0]
    hr = ROW_GROUP
    n_groups = ts // hr
    s_idx = pl.program_id(1)
    d = D_MODEL
    n_gate = N_BRANCH * d // GATE_CHUNK
    n_blk = hr // SG_BLOCK

    @pl.when(s_idx == 0)
    def _():
        halo[...] = jnp.zeros((POOL_HALO, d), F32)

    def prologue(r0):
        h = _layer_norm(x_ref[r0:r0 + hr, :]) * (1.0 + mod_ref[1:2, :]) + mod_ref[0:1, :]
        return h.astype(BF16)

    def gelu(t):
        return 0.5 * t * (1.0 + lax.erf(t * (2.0 ** -0.5)))

    def first_proj(hb):
        return (_dot(hb, w_in_ref[:, C_POOL:C_POOL + d]), _dot(hb, w_in_ref[:, C_LAT:C_LAT + LAT_W]))

    def body(r0, hb, a, lat):
        rows = slice(r0, r0 + hr)

        def proj(start, width):
            return _dot(hb, w_in_ref[:, start:start + width])

        pos = lax.broadcasted_iota(jnp.int32, (hr, 1), 0) + (s_idx * ts + r0)
        pooled = []
        for g, w in enumerate(POOL_WINDOWS):
            cols = slice(g * POOL_GW, (g + 1) * POOL_GW)
            cur = a[:, cols]
            win = jnp.concatenate([halo[:, cols], cur], axis=0)
            span = 1
            while span < w:
                win = win + pltpu.roll(win, span, 0)
                span *= 2
            cnt = jnp.minimum(pos + 1, w).astype(F32)
            pooled.append((win[POOL_HALO:POOL_HALO + hr] / cnt - cur).astype(BF16))
        halo[...] = a[hr - POOL_HALO:hr, :]

        uv_v = [proj(C_UV + d + c * UV_CHUNK, UV_CHUNK) for c in range(d // UV_CHUNK)]

        cos = cos_ref[rows, :]
        sin = sin_ref[rows, :]
        cqn = _rms_norm(lat[:, 0:Q_LORA], g_q_ref[...]).astype(BF16)
        ckvn = _rms_norm(lat[:, Q_LORA:Q_LORA + KV_LORA], g_kv_ref[...])
        o_kr = Q_LORA + KV_LORA
        kr = (lat[:, o_kr:o_kr + 2 * QK_ROPE] * cos
              + lat[:, o_kr + 2 * QK_ROPE:o_kr + 4 * QK_ROPE] * sin)
        k_tile = jnp.concatenate([ckvn.astype(BF16), kr.astype(BF16)], axis=1)
        k_ref[rows, :] = k_tile
        for n in range(hr // TQ_ATT):
            kt_ref[r0 // TQ_ATT + n] = k_tile[n * TQ_ATT:(n + 1) * TQ_ATT, :].T

        mixed = [_dot(pooled[g], w_pool_ref[g]) for g in range(len(POOL_WINDOWS))]
        q_lat = _dot(cqn, w_qn_ref[...])
        q_r = _dot(cqn, w_qr_ref[...])
        q_rs = _dot(cqn, w_qrs_ref[...])

        v = jnp.concatenate([gelu(t) for t in uv_v], axis=1)
        v = (_layer_norm(v) * g_sg_ref[...] + b_sg_ref[...]).astype(BF16)

        uv_u = [proj(C_UV + c * UV_CHUNK, UV_CHUNK) for c in range(d // UV_CHUNK)]

        for g in range(len(POOL_WINDOWS)):
            cols = slice(g * POOL_GW, (g + 1) * POOL_GW)
            ya_ref[rows, cols] = (mixed[g] * s_pool_ref[:, cols]).astype(BF16)
        q_lat = (q_lat * ATT_SCALE).astype(BF16)
        reps = MLA_HEADS * QK_ROPE // cos.shape[1]
        cos_t = jnp.concatenate([cos] * reps, axis=1)
        sin_t = jnp.concatenate([sin] * reps, axis=1)
        q_rope = ((q_r * cos_t + q_rs * sin_t) * ATT_SCALE).astype(BF16)
        lane = lax.broadcasted_iota(jnp.int32, (hr, 2 * QK_ROPE), 1)
        for hd in range(MLA_HEADS):
            pair = q_rope[:, (hd // 2) * 2 * QK_ROPE:(hd // 2 + 1) * 2 * QK_ROPE]
            keep = (lane < QK_ROPE) if hd % 2 == 0 else (lane >= QK_ROPE)
            q_ref[r0 // hr, hd * hr:(hd + 1) * hr, 0:KV_LORA] = q_lat[:, hd * KV_LORA:(hd + 1) * KV_LORA]
            q_ref[r0 // hr, hd * hr:(hd + 1) * hr, KV_LORA:] = jnp.where(keep, pair, jnp.zeros_like(pair))

        gate_z = [proj(C_GATE + c * GATE_CHUNK, GATE_CHUNK) for c in range(2)]

        u = jnp.concatenate([gelu(t) for t in uv_u], axis=1)

        tri_r = lax.broadcasted_iota(jnp.int32, (SG_BLOCK, SG_BLOCK), 0)
        tri_c = lax.broadcasted_iota(jnp.int32, (SG_BLOCK, SG_BLOCK), 1)
        z = []
        for g in range(SG_GROUPS):
            cols = slice(g * SG_GW, (g + 1) * SG_GW)
            w_tril = jnp.where(tri_c <= tri_r, w_s_ref[g], 0.0).astype(BF16)
            v_cat = jnp.concatenate(
                [v[n * SG_BLOCK:(n + 1) * SG_BLOCK, cols] for n in range(n_blk)], axis=1)
            z.append(_dot(w_tril, v_cat))

        def store_gates(c, t):
            gates_ref[rows, c * GATE_CHUNK:(c + 1) * GATE_CHUNK] = jax.nn.sigmoid(t).astype(BF16)

        for c in range(2, n_gate, 2):
            nxt = [proj(C_GATE + cc * GATE_CHUNK, GATE_CHUNK) for cc in range(c, min(c + 2, n_gate))]
            for k, t in enumerate(gate_z):
                store_gates(c - 2 + k, t)
            if c == 2:
                for g in range(SG_GROUPS):
                    cols = slice(g * SG_GW, (g + 1) * SG_GW)
                    for n in range(n_blk):
                        zn = z[g][:, n * SG_GW:(n + 1) * SG_GW] + b_s_ref[:, cols]
                        blk = slice(n * SG_BLOCK, (n + 1) * SG_BLOCK)
                        yc_ref[r0 + n * SG_BLOCK:r0 + (n + 1) * SG_BLOCK, cols] = (
                            u[blk, cols] * zn).astype(BF16)
            gate_z = nxt

        def tail():
            for k, t in enumerate(gate_z):
                store_gates(n_gate - len(gate_z) + k, t)
        return tail

    hb = prologue(0)
    first = first_proj(hb)
    tail = None
    for k in range(n_groups):
        hb_cur, first_cur = hb, first
        if k + 1 < n_groups:
            hb = prologue((k + 1) * hr)
        if tail is not None:
            tail()
        tail = body(k * hr, hb_cur, *first_cur)
        if k + 1 < n_groups:
            first = first_proj(hb)
    tail()


def _mixer_in(x, mod_l, cos, sin, w):
    nb, seq, d = x.shape
    ts = TS_IN
    tile = lambda width: pl.BlockSpec((None, ts, width), lambda b, s: (b, s, 0))
    consts = [w["w_in"], w["w_pool"], w["s_pool"], w["g_q"], w["w_qn"], w["w_qr"], w["w_qrs"],
              w["g_kv"], w["g_sg"], w["b_sg"], w["w_s"], w["b_s"]]
    k_w = KV_LORA + 2 * QK_ROPE
    return pl.pallas_call(
        _mixer_in_kernel,
        grid=(nb, seq // ts),
        in_specs=[tile(d),
                  pl.BlockSpec((None, 6, d), lambda b, s: (b, 0, 0)),
                  tile(cos.shape[-1]), tile(sin.shape[-1])]
                 + [_const_spec(a.shape) for a in consts],
        out_specs=[tile(N_BRANCH * d), tile(d),
                   pl.BlockSpec((None, ts // ROW_GROUP, MLA_HEADS * ROW_GROUP, k_w),
                                lambda b, s: (b, s, 0, 0)),
                   tile(k_w),
                   pl.BlockSpec((None, ts // TQ_ATT, k_w, TQ_ATT), lambda b, s: (b, s, 0, 0)),
                   tile(d)],
        out_shape=[jax.ShapeDtypeStruct((nb, seq, N_BRANCH * d), BF16),
                   jax.ShapeDtypeStruct((nb, seq, d), BF16),
                   jax.ShapeDtypeStruct((nb, seq // ROW_GROUP, MLA_HEADS * ROW_GROUP, k_w), BF16),
                   jax.ShapeDtypeStruct((nb, seq, k_w), BF16),
                   jax.ShapeDtypeStruct((nb, seq // TQ_ATT, k_w, TQ_ATT), BF16),
                   jax.ShapeDtypeStruct((nb, seq, d), BF16)],
        scratch_shapes=[pltpu.VMEM((POOL_HALO, d), F32)],
        compiler_params=pltpu.CompilerParams(
            dimension_semantics=("arbitrary", "arbitrary"),
            vmem_limit_bytes=VMEM_LIMIT_BYTES),
        name="mixer_in",
    )(x, mod_l, cos, sin, *consts)


def _attention_kernel(q_ref, kt_ref, k_ref, o_ref, m_scr, l_scr, acc_scr, s_scr):
    n_grp, g_rows, k_w = q_ref.shape
    rows = n_grp * g_rows
    tq = g_rows // MLA_HEADS
    tk = kt_ref.shape[2]
    i = pl.program_id(1)

    def scores(j, r0=0):
        return _dot(q_ref[...].reshape(rows, k_w)[r0:, :], kt_ref[j])

    shift = CHUNK.bit_length() - 1
    q_chunk = lax.broadcasted_iota(jnp.int32, (tq, tk), 0) >> shift
    k_chunk = lax.broadcasted_iota(jnp.int32, (tq, tk), 1) >> shift
    bias = jnp.where(k_chunk <= q_chunk, 0.0, MASK_VALUE).astype(F32)

    def lane_tiles(s):
        return [s[:, c * LANES:(c + 1) * LANES] for c in range(s.shape[1] // LANES)]

    def row_max(s):
        part = functools.reduce(jnp.maximum, lane_tiles(s))
        return jnp.broadcast_to(jnp.max(part, axis=1, keepdims=True), (s.shape[0], LANES))

    def value_block(j):
        return k_ref[pl.ds(pl.multiple_of(j * tk, tk), tk), 0:KV_LORA]

    def diag_scores(g):
        s = scores(n_grp * i + g, g * g_rows)
        first = s[0:g_rows].reshape(MLA_HEADS, tq, tk) + bias[None]
        first = first.reshape(g_rows, tk)
        return first if g == n_grp - 1 else jnp.concatenate([first, s[g_rows:]], axis=0)

    def update(j, s, r0=0):
        m_old = m_scr[r0:, :]
        m_new = jnp.maximum(m_old, row_max(s))
        alpha = jnp.exp2(m_old - m_new)
        p_tiles = [jnp.exp2(t - m_new) for t in lane_tiles(s)]
        l_scr[r0:, :] = alpha * l_scr[r0:, :] + functools.reduce(jnp.add, p_tiles)
        p = jnp.concatenate(p_tiles, axis=1).astype(BF16)
        acc_scr[r0:, :] = alpha * acc_scr[r0:, :] + _dot(p, value_block(j))
        m_scr[r0:, :] = m_new

    s0 = diag_scores(0)
    s_scr[...] = scores(0)
    m0 = row_max(s0)
    p_tiles = [jnp.exp2(t - m0) for t in lane_tiles(s0)]
    m_scr[...] = m0
    l_scr[...] = functools.reduce(jnp.add, p_tiles)
    acc_scr[...] = _dot(jnp.concatenate(p_tiles, axis=1).astype(BF16), value_block(n_grp * i))
    for g in range(1, n_grp):
        update(n_grp * i + g, diag_scores(g), g * g_rows)

    n_full = n_grp * i

    @pl.when(n_full > 0)
    def _():
        def body(j, carry):
            s_next = scores(j + 1)
            update(j, s_scr[...])
            s_scr[...] = s_next
            return carry

        lax.fori_loop(0, n_full - 1, body, 0)
        update(n_full - 1, s_scr[...])

    o_lat = (acc_scr[...] / jnp.sum(l_scr[...], axis=1, keepdims=True)).astype(BF16)
    for g in range(n_grp):
        for hd in range(MLA_HEADS):
            r0 = g * g_rows + hd * tq
            o_ref[g * tq:(g + 1) * tq, hd * KV_LORA:(hd + 1) * KV_LORA] = o_lat[r0:r0 + tq, :]


def _attention(q, k, kt):
    nb, n_tiles, g_rows, k_w = q.shape
    seq = k.shape[1]
    tq = g_rows // MLA_HEADS
    n_grp = ATT_GROUPS
    rows = n_grp * g_rows
    return pl.pallas_call(
        _attention_kernel,
        grid=(nb, n_tiles // n_grp),
        in_specs=[pl.BlockSpec((None, n_grp, g_rows, k_w), lambda b, i: (b, i, 0, 0)),
                  pl.BlockSpec((None, seq // tq, k_w, tq), lambda b, i: (b, 0, 0, 0)),
                  pl.BlockSpec((None, seq, k_w), lambda b, i: (b, 0, 0))],
        out_specs=pl.BlockSpec((None, n_grp * tq, MLA_HEADS * KV_LORA), lambda b, i: (b, i, 0)),
        out_shape=jax.ShapeDtypeStruct((nb, seq, MLA_HEADS * KV_LORA), BF16),
        scratch_shapes=[pltpu.VMEM((rows, LANES), F32),
                        pltpu.VMEM((rows, LANES), F32),
                        pltpu.VMEM((rows, KV_LORA), F32),
                        pltpu.VMEM((rows, tq), F32)],
        compiler_params=pltpu.CompilerParams(
            dimension_semantics=("arbitrary", "arbitrary"),
            vmem_limit_bytes=VMEM_LIMIT_BYTES),
        name="latent_attention",
    )(q, kt, k)


def _mixer_out_kernel(x_ref, mod_ref, gates_ref, ya_ref, yb_ref, yc_ref, w_br_ref, w_o_ref,
                      ln_g_ref, ln_b_ref, o_ref, *, alpha):
    tm, d = x_ref.shape
    hr = ROW_GROUP
    n_groups = tm // hr
    branches = (ya_ref, yb_ref, yc_ref)

    def branch_dots(r0):
        return [_dot(y_ref[r0:r0 + hr, :], w_br_ref[br]) for br, y_ref in enumerate(branches)]

    def merge(r0, ys):
        merged = None
        for br, y in enumerate(ys):
            term = gates_ref[r0:r0 + hr, br * d:(br + 1) * d].astype(F32) * y
            merged = term if merged is None else merged + term
        return merged.astype(BF16)

    def epilogue(r0, y):
        r = alpha * x_ref[r0:r0 + hr, :] + mod_ref[2:3, :] * y
        o_ref[r0:r0 + hr, :] = _layer_norm(r) * ln_g_ref[...] + ln_b_ref[...]

    ys = branch_dots(0)
    pending = None
    for k in range(n_groups):
        ys_cur = ys
        if k + 1 < n_groups:
            ys = branch_dots((k + 1) * hr)
        merged = merge(k * hr, ys_cur)
        y = _dot(merged, w_o_ref[...])
        if pending is not None:
            epilogue(*pending)
        pending = (k * hr, y)
    epilogue(*pending)


def _mixer_out(x, mod_l, gates, ya, yb, yc, w_br, w_o, ln_g, ln_b, alpha):
    nb, seq, d = x.shape
    tm = TM_OUT
    tile = lambda width: pl.BlockSpec((None, tm, width), lambda b, s: (b, s, 0))
    return pl.pallas_call(
        functools.partial(_mixer_out_kernel, alpha=alpha),
        grid=(nb, seq // tm),
        in_specs=[tile(d),
                  pl.BlockSpec((None, 6, d), lambda b, s: (b, 0, 0)),
                  tile(N_BRANCH * d), tile(d), tile(d), tile(d),
                  _const_spec(w_br.shape), _const_spec(w_o.shape),
                  _const_spec(ln_g.shape), _const_spec(ln_b.shape)],
        out_specs=tile(d),
        out_shape=jax.ShapeDtypeStruct((nb, seq, d), F32),
        compiler_params=pltpu.CompilerParams(
            dimension_semantics=("arbitrary", "arbitrary"),
            vmem_limit_bytes=VMEM_LIMIT_BYTES),
        name="mixer_out",
    )(x, mod_l, gates, ya, yb, yc, w_br, w_o, ln_g, ln_b)


def _conv_ffn_kernel(x_ref, mod_ref, w_up_ref, cw_ref, cb_ref, w_down_ref, ln_g_ref, ln_b_ref,
                     o_ref, zhalo, *, alpha):
    ts = x_ref.shape[0]
    hr = ROW_GROUP
    n_groups = ts // hr
    s_idx = pl.program_id(1)
    bounds = list(range(0, D_FF, FFN_CHUNK)) + [D_FF]
    n_chunks = len(bounds) - 1

    @pl.when(s_idx == 0)
    def _():
        zhalo[...] = jnp.zeros((CONV_HALO, 2 * D_FF), F32)

    def prologue(r0):
        h = _layer_norm(x_ref[r0:r0 + hr, :]) * (1.0 + mod_ref[4:5, :]) + mod_ref[3:4, :]
        return h.astype(BF16)

    def chunk_cols(c):
        return (slice(bounds[c], bounds[c + 1]), slice(D_FF + bounds[c], D_FF + bounds[c + 1]))

    def up(hb, c):
        return [_dot(hb, w_up_ref[:, cols]) for cols in chunk_cols(c)]

    def conv(z, cols):
        ext = jnp.concatenate([zhalo[:, cols], z], axis=0)
        zc = cb_ref[:, cols] + cw_ref[CONV_W - 1:CONV_W, cols] * z
        for k in range(1, CONV_W):
            shifted = pltpu.roll(ext, k, 0)[CONV_HALO:CONV_HALO + hr]
            zc = zc + cw_ref[CONV_W - 1 - k:CONV_W - k, cols] * shifted
        zhalo[:, cols] = z[hr - CONV_HALO:hr]
        return zc

    def down(c, act):
        return _dot(act, w_down_ref[bounds[c]:bounds[c + 1], :])

    def epilogue(r0, y):
        r = alpha * x_ref[r0:r0 + hr, :] + mod_ref[5:6, :] * y
        o_ref[r0:r0 + hr, :] = _layer_norm(r) * ln_g_ref[...] + ln_b_ref[...]

    hb = prologue(0)
    z_next = up(hb, 0)
    pending = None
    for k in range(n_groups):
        hb_cur = hb
        if k + 1 < n_groups:
            hb = prologue((k + 1) * hr)
        y = jnp.zeros((hr, D_MODEL), F32)
        act_prev = None
        for c in range(n_chunks):
            z_cur = z_next
            if c + 1 < n_chunks:
                z_next = up(hb_cur, c + 1)
            elif k + 1 < n_groups:
                z_next = up(hb, 0)
            if act_prev is not None:
                y = y + down(c - 1, act_prev)
            if c == 0 and pending is not None:
                epilogue(*pending)
            val, gate = [conv(z, cols) for z, cols in zip(z_cur, chunk_cols(c))]
            act_prev = (gate * jax.nn.sigmoid(gate) * val).astype(BF16)
        y = y + down(n_chunks - 1, act_prev)
        pending = (k * hr, y)
    epilogue(*pending)


def _conv_ffn(x, mod_l, w_up, conv_w, conv_b, w_down, ln_g, ln_b, alpha):
    nb, seq, d = x.shape
    ts = TS_FFN
    tile = pl.BlockSpec((None, ts, d), lambda b, s: (b, s, 0))
    consts = [w_up, conv_w, conv_b, w_down, ln_g, ln_b]
    return pl.pallas_call(
        functools.partial(_conv_ffn_kernel, alpha=alpha),
        grid=(nb, seq // ts),
        in_specs=[tile, pl.BlockSpec((None, 6, d), lambda b, s: (b, 0, 0))]
                 + [_const_spec(a.shape) for a in consts],
        out_specs=tile,
        out_shape=jax.ShapeDtypeStruct((nb, seq, d), F32),
        scratch_shapes=[pltpu.VMEM((CONV_HALO, 2 * D_FF), F32)],
        compiler_params=pltpu.CompilerParams(
            dimension_semantics=("arbitrary", "arbitrary"),
            vmem_limit_bytes=VMEM_LIMIT_BYTES),
        name="conv_ffn",
    )(x, mod_l, w_up, conv_w, conv_b, w_down, ln_g, ln_b)


def _swap_halves(w):
    half = w.shape[-1] // 2
    return jnp.concatenate([w[..., half:], w[..., :half]], axis=-1)


def _rope_tables(pos):
    half = QK_ROPE // 2
    freqs = ROPE_THETA ** (-jnp.arange(0, QK_ROPE, 2, dtype=F32) / QK_ROPE)
    ang = pos.astype(F32)[..., None] * jnp.tile(freqs, LANES // half)
    sign = jnp.tile(jnp.concatenate([-jnp.ones((half,), F32), jnp.ones((half,), F32)]),
                    LANES // QK_ROPE)
    return jnp.cos(ang), jnp.sin(ang) * sign


def _fold_kernel(a_ref, b_ref, o_ref):
    o_ref[...] = jnp.dot(a_ref[...], b_ref[...], preferred_element_type=F32,
                         precision=lax.Precision.HIGHEST).astype(o_ref.dtype)


def _fold_heads(a, b):
    n, m, kk = a.shape
    nn = b.shape[2]
    return pl.pallas_call(
        _fold_kernel,
        grid=(n,),
        in_specs=[pl.BlockSpec((None, m, kk), lambda h: (h, 0, 0)),
                  pl.BlockSpec((None, kk, nn), lambda h: (h, 0, 0))],
        out_specs=pl.BlockSpec((None, m, nn), lambda h: (h, 0, 0)),
        out_shape=jax.ShapeDtypeStruct((n, m, nn), BF16),
        name="fold_weights",
    )(a, b)


def _absorbed_weights(w_uq, w_ukv, w_branch):
    depth = w_uq.shape[0]
    h, d = MLA_HEADS, D_MODEL
    uq_nope = jnp.transpose(w_uq[..., :QK_NOPE], (0, 2, 1, 3)).reshape(depth * h, Q_LORA, QK_NOPE)
    uk_t = jnp.transpose(w_ukv[..., :QK_NOPE], (0, 2, 3, 1)).reshape(depth * h, QK_NOPE, KV_LORA)
    w_q = _fold_heads(uq_nope, uk_t).reshape(depth, h, Q_LORA, KV_LORA)
    w_q = jnp.transpose(w_q, (0, 2, 1, 3)).reshape(depth, Q_LORA, h * KV_LORA)
    uv = jnp.transpose(w_ukv[..., QK_NOPE:], (0, 2, 1, 3)).reshape(depth * h, KV_LORA, V_DIM)
    w_b = _fold_heads(uv, w_branch[:, 1].reshape(depth * h, V_DIM, d)).reshape(depth, h * KV_LORA, d)
    return w_q, w_b


def _layer_weights(l, w_in, w_pool, s_pool, g_q, w_uq, g_kv, g_sg, b_sg, w_s, b_s, w_q_abs):
    d = D_MODEL
    off_cq = (N_BRANCH + 1) * d
    off_ckv = off_cq + Q_LORA
    off_kr = off_ckv + KV_LORA
    off_sg = off_kr + QK_ROPE
    wi = w_in[l]
    w_kr = wi[:, off_kr:off_sg]
    w_kr_sw = _swap_halves(w_kr)
    w_in_r = jnp.concatenate(
        [wi[:, :off_kr], w_kr, w_kr, w_kr_sw, w_kr_sw, wi[:, off_sg:]], axis=1).astype(BF16)
    w_q_rope = w_uq[l][:, :, QK_NOPE:]
    return dict(
        w_in=w_in_r,
        w_pool=w_pool[l].astype(BF16),
        s_pool=s_pool[l].reshape(1, d),
        g_q=g_q[l].reshape(1, Q_LORA),
        w_qn=w_q_abs[l],
        w_qr=w_q_rope.reshape(Q_LORA, MLA_HEADS * QK_ROPE).astype(BF16),
        w_qrs=_swap_halves(w_q_rope).reshape(Q_LORA, MLA_HEADS * QK_ROPE).astype(BF16),
        g_kv=g_kv[l].reshape(1, KV_LORA),
        g_sg=g_sg[l].reshape(1, d),
        b_sg=b_sg[l].reshape(1, d),
        w_s=w_s[l],
        b_s=jnp.repeat(b_s[l], SG_GW, axis=1),
    )


def kernel(x, c, pos, w_ada, b_ada, w_in, w_pool, s_pool, g_q, w_uq, g_kv, w_ukv, g_sg, b_sg, w_s, b_s, w_branch, w_o, ln_t_g, ln_t_b, w_up, conv_w, conv_b, w_down, ln_f_g, ln_f_b):
    depth = w_ada.shape[0]
    d = D_MODEL
    alpha = (2 * depth) ** 0.25
    assert x.shape[-1] == d and x.shape[1] % TM_OUT == 0
    cos, sin = _rope_tables(pos)
    mod = _modulation(c, w_ada, b_ada)
    w_q_abs, w_b_abs = _absorbed_weights(w_uq, w_ukv, w_branch)
    for l in range(depth):
        w = _layer_weights(l, w_in, w_pool, s_pool, g_q, w_uq, g_kv, g_sg, b_sg, w_s, b_s, w_q_abs)
        gates, ya, q, k, kt, yc = _mixer_in(x, mod[l], cos, sin, w)
        yb = _attention(q, k, kt)
        w_br = jnp.stack([w_branch[l, 0].astype(BF16), w_b_abs[l], w_branch[l, 2].astype(BF16)])
        x = _mixer_out(x, mod[l], gates, ya, yb, yc, w_br,
                       w_o[l].astype(BF16), ln_t_g[l].reshape(1, d), ln_t_b[l].reshape(1, d), alpha)
        x = _conv_ffn(x, mod[l], w_up[l].astype(BF16), conv_w[l], conv_b[l].reshape(1, 2 * D_FF),
                      w_down[l].astype(BF16), ln_f_g[l].reshape(1, d), ln_f_b[l].reshape(1, d), alpha)
    return x
```
